```python
import math, functools
import jax, jax.numpy as jnp
from jax import lax
import numpy as np

D_MODEL = 2048
BATCH = 1
SEQ = 8192
DEPTH = 2
DEC_BATCH = 8
DEC_SEQ = 16
PAST_LEN = 4096

CHUNK = 64
QBLOCK = 128
A_HEADS = 8
A_HEAD_DIM = 64
A_WIDTH = A_HEADS * 2 * A_HEAD_DIM
B_HEADS = 8
B_HEAD_DIM = 128
B_WIDTH = B_HEADS * B_HEAD_DIM
BAND_CHUNKS = 8
REL_CLIP = 128
C_HEADS = 8
C_HEAD_DIM = 128
C_WIDTH = C_HEADS * C_HEAD_DIM
FORGET_BIAS_INIT = 2.0
IN_COLS = 3 * A_WIDTH + 3 * B_WIDTH + 3 * C_WIDTH + C_HEADS
N_BRANCHES = 3
N_EXPERTS = 32
TOP_K = 4
MOE_FF = D_MODEL
SWIGLU_LIMIT = 7.0
SWIGLU_ALPHA = 1.702
MOE_BLOCK = 128

ROPE_THETA = 10000.0
LN_EPS = 1e-5
NEG_INF = -1e30
F32 = jnp.float32

kernel_name = 'hybrid_stream_encoder_step'


def _layer_norm(x, g, b):
    xf = x.astype(F32)
    mu = jnp.mean(xf, axis=-1, keepdims=True)
    var = jnp.mean(jnp.square(xf - mu), axis=-1, keepdims=True)
    return ((xf - mu) * lax.rsqrt(var + LN_EPS) * g + b).astype(x.dtype)


def _rms_norm(x, g):
    xf = x.astype(F32)
    return (xf * lax.rsqrt(jnp.mean(xf * xf, axis=-1, keepdims=True) + LN_EPS) * g).astype(x.dtype)


def _masked_softmax(s, mask):
    return jax.nn.softmax(jnp.where(mask, s.astype(F32), NEG_INF), axis=-1)


def _rotary(x, pos):
    d = x.shape[-1]
    half = d // 2
    inv = ROPE_THETA ** (-jnp.arange(half, dtype=F32) / half)
    ang = pos.astype(F32)[:, None] * inv[None, :]
    shp = (pos.shape[0],) + (1,) * (x.ndim - 3) + (half,)
    cos, sin = jnp.cos(ang).reshape(shp), jnp.sin(ang).reshape(shp)
    xf = x.astype(F32)
    x1, x2 = xf[..., :half], xf[..., half:]
    return jnp.concatenate([x1 * cos - x2 * sin, x2 * cos + x1 * sin], axis=-1).astype(x.dtype)


def _sweep_query_blocks(fn, q_arrays, q_pos):
    n_blk = q_pos.shape[0] // QBLOCK
    def to_blocks(a):
        return jnp.moveaxis(a.reshape((a.shape[0], n_blk, QBLOCK) + a.shape[2:]), 1, 0)
    xs = tuple(to_blocks(a) for a in q_arrays) + (q_pos.reshape(n_blk, QBLOCK),)
    out = jnp.moveaxis(lax.map(lambda args: fn(*args), xs), 0, 1)
    return out.reshape((out.shape[0], n_blk * QBLOCK) + out.shape[3:])


def _project(x, pos, w_in_l, b_forget_l):
    bt, t, _ = x.shape
    sizes = (A_WIDTH,) * 3 + (B_WIDTH,) * 3 + (C_WIDTH,) * 3
    cuts = [int(c) for c in np.cumsum(sizes)]
    a_q, a_k, a_v, b_q, b_k, b_v, c_q, c_k, c_v, c_f = jnp.split(x @ w_in_l, cuts, axis=-1)
    a_q = _rotary(a_q.reshape(bt, t, A_HEADS, 2, A_HEAD_DIM), pos)
    a_k = _rotary(a_k.reshape(bt, t, A_HEADS, 2, A_HEAD_DIM), pos).reshape(bt, t, A_HEADS, 2 * A_HEAD_DIM)
    a_v = a_v.reshape(bt, t, A_HEADS, 2 * A_HEAD_DIM)
    b_q = b_q.reshape(bt, t, B_HEADS, B_HEAD_DIM)
    b_k = b_k.reshape(bt, t, B_HEADS, B_HEAD_DIM)
    b_v = b_v.reshape(bt, t, B_HEADS, B_HEAD_DIM)
    c_q = c_q.reshape(bt, t, C_HEADS, C_HEAD_DIM)
    c_k = c_k.reshape(bt, t, C_HEADS, C_HEAD_DIM)
    c_v = c_v.reshape(bt, t, C_HEADS, C_HEAD_DIM)
    c_logf = jax.nn.log_sigmoid((c_f + b_forget_l).astype(F32))
    return a_q, a_k, a_v, b_q, b_k, b_v, c_q, c_k, c_v, c_logf


def _diff_attention(q, q_pos, k, v, k_pos, lam):
    scale = A_HEAD_DIM ** -0.5
    mask = (k_pos[None, :] // CHUNK) <= (q_pos[:, None] // CHUNK)
    s1 = jnp.einsum('bqhd,bkhd->bhqk', q[..., 0, :], k[..., :A_HEAD_DIM]) * scale
    s2 = jnp.einsum('bqhd,bkhd->bhqk', q[..., 1, :], k[..., A_HEAD_DIM:]) * scale
    w = _masked_softmax(s1, mask) - lam * _masked_softmax(s2, mask)
    return jnp.einsum('bhqk,bkhe->bqhe', w.astype(v.dtype), v)


def _band_attention(q, q_pos, k, v, k_pos, rel_bias_l):
    scale = B_HEAD_DIM ** -0.5
    s = jnp.einsum('bqhd,bkhd->bhqk', q, k).astype(F32) * scale
    rel = jnp.clip(q_pos[:, None] - k_pos[None, :], -REL_CLIP, REL_CLIP) + REL_CLIP
    s = s + rel_bias_l[:, rel].astype(F32)
    qc, kc = q_pos[:, None] // CHUNK, k_pos[None, :] // CHUNK
    mask = (kc <= qc) & (kc >= qc - BAND_CHUNKS) & (k_pos[None, :] >= 0)
    p = _masked_softmax(s, mask)
    return jnp.einsum('bhqk,bkhe->bqhe', p.astype(v.dtype), v)


def _chunk_band_prompt(q, k, v, rel_bias_l):
    bt, s_len, h, d = q.shape
    n_c = s_len // CHUNK
    pad = ((0, 0), (BAND_CHUNKS * CHUNK, 0), (0, 0), (0, 0))
    kp = jnp.pad(k, pad).reshape(bt, n_c + BAND_CHUNKS, CHUNK, h, d)
    vp = jnp.pad(v, pad).reshape(bt, n_c + BAND_CHUNKS, CHUNK, h, d)
    kb = jnp.concatenate([kp[:, i:i + n_c] for i in range(BAND_CHUNKS + 1)], axis=2)
    vb = jnp.concatenate([vp[:, i:i + n_c] for i in range(BAND_CHUNKS + 1)], axis=2)
    q_pos = jnp.arange(s_len).reshape(n_c, CHUNK)
    k_pos = (jnp.arange(n_c)[:, None] - BAND_CHUNKS) * CHUNK + jnp.arange((BAND_CHUNKS + 1) * CHUNK)[None, :]
    out = jax.vmap(_band_attention, in_axes=(1, 0, 1, 1, 0, None), out_axes=1)(
        q.reshape(bt, n_c, CHUNK, h, d), q_pos, kb, vb, k_pos, rel_bias_l)
    return out.reshape(bt, s_len, h, d)


def _forgetting_attention(q, cum_q, q_pos, k, v, cum_k, k_pos):
    scale = C_HEAD_DIM ** -0.5
    s = jnp.einsum('bqhd,bkhd->bhqk', q, k).astype(F32) * scale
    decay = jnp.swapaxes(cum_q, 1, 2)[..., :, None] - jnp.swapaxes(cum_k, 1, 2)[..., None, :]
    mask = k_pos[None, :] <= q_pos[:, None]
    p = _masked_softmax(s + decay, mask)
    return jnp.einsum('bhqk,bkhe->bqhe', p.astype(v.dtype), v)


def _mixers_prompt(h, pos, w_in_l, b_forget_l, lam, rel_bias_l):
    a_q, a_k, a_v, b_q, b_k, b_v, c_q, c_k, c_v, c_logf = _project(h, pos, w_in_l, b_forget_l)
    o_a = _sweep_query_blocks(functools.partial(_diff_attention, k=a_k, v=a_v, k_pos=pos, lam=lam), (a_q,), pos)
    o_b = _chunk_band_prompt(b_q, b_k, b_v, rel_bias_l)
    cum = jnp.cumsum(c_logf, axis=1)
    o_c = _sweep_query_blocks(functools.partial(_forgetting_attention, k=c_k, v=c_v, cum_k=cum, k_pos=pos),
                              (c_q, cum), pos)
    band = min(BAND_CHUNKS * CHUNK, h.shape[1])
    state = (a_k, a_v, b_k[:, -band:], b_v[:, -band:], c_k, c_v, c_logf.astype(h.dtype))
    return o_a, o_b, o_c, state


def _mixers_sample(h, pos, ca_k, ca_v, cb_k, cb_v, cc_k, cc_v, cc_logf, w_in_l, b_forget_l, lam, rel_bias_l):
    past, band, t = ca_k.shape[1], cb_k.shape[1], h.shape[1]
    a_q, a_k, a_v, b_q, b_k, b_v, c_q, c_k, c_v, c_logf = _project(h, pos, w_in_l, b_forget_l)
    k_pos_full = jnp.arange(past + t)
    o_a = _diff_attention(a_q, pos, jnp.concatenate([ca_k, a_k], axis=1), jnp.concatenate([ca_v, a_v], axis=1),
                          k_pos_full, lam)
    o_b = _band_attention(b_q, pos, jnp.concatenate([cb_k, b_k], axis=1), jnp.concatenate([cb_v, b_v], axis=1),
                          past - band + jnp.arange(band + t), rel_bias_l)
    cum = jnp.cumsum(jnp.concatenate([cc_logf.astype(F32), c_logf], axis=1), axis=1)
    o_c = _forgetting_attention(c_q, cum[:, past:], pos, jnp.concatenate([cc_k, c_k], axis=1),
                                jnp.concatenate([cc_v, c_v], axis=1), cum, k_pos_full)
    state = (a_k, a_v, b_k, b_v, c_k, c_v, c_logf.astype(h.dtype))
    return o_a, o_b, o_c, state


def _merge(x, o_a, o_b, o_c, subln_g_l, lam_init, w_branch_a_l, w_branch_b_l, w_branch_c_l, w_gate_l, b_gate_l, w_out_l):
    bt, t, d = x.shape
    o_a = _rms_norm(o_a, subln_g_l) * (1.0 - lam_init)
    br_a = o_a.reshape(bt, t, A_WIDTH) @ w_branch_a_l
    br_b = o_b.reshape(bt, t, B_WIDTH) @ w_branch_b_l
    br_c = o_c.reshape(bt, t, C_WIDTH) @ w_branch_c_l
    g = jax.nn.sigmoid((x @ w_gate_l + b_gate_l).astype(F32)).astype(x.dtype).reshape(bt, t, N_BRANCHES, d)
    mixed = g[..., 0, :] * br_a + g[..., 1, :] * br_b + g[..., 2, :] * br_c
    return mixed @ w_out_l


def _moe(x, w_router_l, b_router_l, w_gate_up_l, b_gate_up_l, w_down_l, b_down_l):
    bt, t, d = x.shape
    h = x.reshape(bt * t, d)
    n_tok = h.shape[0]
    logits = (h @ w_router_l + b_router_l).astype(F32)
    top_logit, top_idx = lax.top_k(logits, TOP_K)
    gate = jax.nn.softmax(top_logit, axis=-1)
    n_assign = n_tok * TOP_K
    e_flat = top_idx.reshape(n_assign)
    tok_flat = jnp.repeat(jnp.arange(n_tok, dtype=jnp.int32), TOP_K)
    order = jnp.argsort(e_flat)
    e_sorted = e_flat[order]
    counts = jnp.bincount(e_flat, length=N_EXPERTS)
    padded = (counts + MOE_BLOCK - 1) // MOE_BLOCK * MOE_BLOCK
    start = jnp.cumsum(counts) - counts
    pend = jnp.cumsum(padded)
    pstart = pend - padded
    dest = pstart[e_sorted] + (jnp.arange(n_assign) - start[e_sorted])
    n_blocks = (n_assign + N_EXPERTS * (MOE_BLOCK - 1)) // MOE_BLOCK + 1
    n_rows = n_blocks * MOE_BLOCK
    row_tok = jnp.full((n_rows,), n_tok, dtype=jnp.int32).at[dest].set(tok_flat[order])
    row_gate = jnp.zeros((n_rows,), dtype=h.dtype).at[dest].set(gate.reshape(n_assign)[order].astype(h.dtype))
    block_expert = jnp.minimum(jnp.searchsorted(pend, jnp.arange(n_blocks) * MOE_BLOCK, side='right'),
                               N_EXPERTS - 1)
    h_pad = jnp.concatenate([h, jnp.zeros((1, d), h.dtype)], axis=0)

    def run_block(args):
        tok, e = args
        gu = h_pad[tok] @ w_gate_up_l[e] + b_gate_up_l[e]
        g_h = jnp.minimum(gu[:, :MOE_FF], SWIGLU_LIMIT)
        u_h = jnp.clip(gu[:, MOE_FF:], -SWIGLU_LIMIT, SWIGLU_LIMIT)
        act = g_h * jax.nn.sigmoid(SWIGLU_ALPHA * g_h) * (u_h + 1.0)
        return act @ w_down_l[e] + b_down_l[e]

    y_rows = lax.map(run_block, (row_tok.reshape(n_blocks, MOE_BLOCK), block_expert))
    y_rows = y_rows.reshape(n_rows, d) * row_gate[:, None]
    out = jax.ops.segment_sum(y_rows, row_tok, num_segments=n_tok + 1)[:n_tok]
    return out.reshape(bt, t, d)


def setup_inputs(seed: int = 0) -> dict:
    key = jax.random.key(seed)
    ks = iter(jax.random.split(key, 40))
    nrm = lambda shape, s=1.0: jax.random.normal(next(ks), shape, F32) * s
    beta = (8.0 * DEPTH) ** -0.25
    band = min(BAND_CHUNKS * CHUNK, PAST_LEN)
    D = D_MODEL
    return {
        'x_prompt': nrm((BATCH, SEQ, D)),
        'x_sample': nrm((DEC_BATCH, DEC_SEQ, D)),
        'cache_a_k': nrm((DEPTH, DEC_BATCH, PAST_LEN, A_HEADS, 2 * A_HEAD_DIM)),
        'cache_a_v': nrm((DEPTH, DEC_BATCH, PAST_LEN, A_HEADS, 2 * A_HEAD_DIM)),
        'cache_b_k': nrm((DEPTH, DEC_BATCH, band, B_HEADS, B_HEAD_DIM)),
        'cache_b_v': nrm((DEPTH, DEC_BATCH, band, B_HEADS, B_HEAD_DIM)),
        'cache_c_k': nrm((DEPTH, DEC_BATCH, PAST_LEN, C_HEADS, C_HEAD_DIM)),
        'cache_c_v': nrm((DEPTH, DEC_BATCH, PAST_LEN, C_HEADS, C_HEAD_DIM)),
        'cache_c_logf': jax.nn.log_sigmoid(FORGET_BIAS_INIT + nrm((DEPTH, DEC_BATCH, PAST_LEN, C_HEADS))),
        'ln_in_g': 1.0 + nrm((D,), 0.02),
        'ln_in_b': nrm((D,), 0.02),
        'w_in': nrm((DEPTH, D, IN_COLS), D ** -0.5),
        'b_forget': FORGET_BIAS_INIT + nrm((DEPTH, C_HEADS), 0.5),
        'diff_lambda': nrm((DEPTH, 4, A_HEAD_DIM), 0.1),
        'diff_subln_g': 1.0 + nrm((DEPTH, 2 * A_HEAD_DIM), 0.02),
        'rel_bias': nrm((DEPTH, B_HEADS, 2 * REL_CLIP + 1), 0.5),
        'w_branch_a': nrm((DEPTH, A_WIDTH, D), A_WIDTH ** -0.5),
        'w_branch_b': nrm((DEPTH, B_WIDTH, D), B_WIDTH ** -0.5),
        'w_branch_c': nrm((DEPTH, C_WIDTH, D), C_WIDTH ** -0.5),
        'w_gate': nrm((DEPTH, D, N_BRANCHES * D), D ** -0.5),
        'b_gate': nrm((DEPTH, N_BRANCHES * D), 0.02),
        'w_out': nrm((DEPTH, D, D), beta * D ** -0.5),
        'ln_mix_g': 1.0 + nrm((DEPTH, D), 0.02),
        'ln_mix_b': nrm((DEPTH, D), 0.02),
        'w_router': nrm((DEPTH, D, N_EXPERTS), D ** -0.5),
        'b_router': nrm((DEPTH, N_EXPERTS), 0.01),
        'w_gate_up': nrm((DEPTH, N_EXPERTS, D, 2 * MOE_FF), D ** -0.5),
        'b_gate_up': nrm((DEPTH, N_EXPERTS, 2 * MOE_FF), 0.02),
        'w_down': nrm((DEPTH, N_EXPERTS, MOE_FF, D), beta * MOE_FF ** -0.5),
        'b_down': nrm((DEPTH, N_EXPERTS, D), 0.02),
        'ln_ffn_g': 1.0 + nrm((DEPTH, D), 0.02),
        'ln_ffn_b': nrm((DEPTH, D), 0.02),
    }


def reference(x_prompt, x_sample, cache_a_k, cache_a_v, cache_b_k, cache_b_v, cache_c_k, cache_c_v, cache_c_logf,
              ln_in_g, ln_in_b, w_in, b_forget, diff_lambda, diff_subln_g, rel_bias, w_branch_a, w_branch_b,
              w_branch_c, w_gate, b_gate, w_out, ln_mix_g, ln_mix_b, w_router, b_router, w_gate_up, b_gate_up,
              w_down, b_down, ln_ffn_g, ln_ffn_b):
    alpha = (2.0 * DEPTH) ** 0.25
    past = cache_a_k.shape[2]
    pos_p = jnp.arange(x_prompt.shape[1])
    pos_s = past + jnp.arange(x_sample.shape[1])
    hp = _layer_norm(x_prompt, ln_in_g, ln_in_b)
    hs = _layer_norm(x_sample, ln_in_g, ln_in_b)
    p_states, s_states = [], []
    for l in range(DEPTH):
        lam_init = 0.8 - 0.6 * math.exp(-0.3 * l)
        dl = diff_lambda[l].astype(F32)
        lam = jnp.exp(jnp.sum(dl[0] * dl[1])) - jnp.exp(jnp.sum(dl[2] * dl[3])) + lam_init
        merge_w = (diff_subln_g[l], lam_init, w_branch_a[l], w_branch_b[l], w_branch_c[l], w_gate[l], b_gate[l], w_out[l])
        moe_w = (w_router[l], b_router[l], w_gate_up[l], b_gate_up[l], w_down[l], b_down[l])
        o_a, o_b, o_c, st = _mixers_prompt(hp, pos_p, w_in[l], b_forget[l], lam, rel_bias[l])
        hp = _layer_norm(alpha * hp + _merge(hp, o_a, o_b, o_c, *merge_w), ln_mix_g[l], ln_mix_b[l])
        hp = _layer_norm(alpha * hp + _moe(hp, *moe_w), ln_ffn_g[l], ln_ffn_b[l])
        p_states.append(st)
        o_a, o_b, o_c, st = _mixers_sample(hs, pos_s, cache_a_k[l], cache_a_v[l], cache_b_k[l], cache_b_v[l],
                                           cache_c_k[l], cache_c_v[l], cache_c_logf[l], w_in[l], b_forget[l],
                                           lam, rel_bias[l])
        hs = _layer_norm(alpha * hs + _merge(hs, o_a, o_b, o_c, *merge_w), ln_mix_g[l], ln_mix_b[l])
        hs = _layer_norm(alpha * hs + _moe(hs, *moe_w), ln_ffn_g[l], ln_ffn_b[l])
        s_states.append(st)
    p = [jnp.stack([st[i] for st in p_states], axis=0) for i in range(7)]
    s = [jnp.stack([st[i] for st in s_states], axis=0) for i in range(7)]
    return (hp, hs, p[0], p[1], p[2], p[3], p[4], p[5], p[6], s[0], s[1], s[2], s[3], s[4], s[5], s[6])
```

```python
import functools
import math

import numpy as np
import jax
import jax.numpy as jnp
from jax import lax
from jax.experimental import pallas as pl
from jax.experimental.pallas import tpu as pltpu

F32 = jnp.float32
BF16 = jnp.bfloat16
I32 = jnp.int32

CHUNK = 64
N_HEADS = 8
HEAD_W = 128
WIDTH = N_HEADS * HEAD_W
A_HEAD_DIM = 64
BAND_CHUNKS = 8
REL_CLIP = 128
N_BRANCHES = 3
N_SEG = 9
N_EXPERTS = 32
TOP_K = 4
SWIGLU_LIMIT = 7.0
SWIGLU_ALPHA = 1.702
ROPE_THETA = 10000.0
LN_EPS = 1e-5
NEG_INF = -1e30
VMEM_LIMIT_BYTES = 56 * 1024 * 1024

SEG_AQ, SEG_AK, SEG_AV, SEG_BQ, SEG_BK, SEG_BV, SEG_CQ, SEG_CK, SEG_CV = range(N_SEG)


def _params(sem):
    return pltpu.CompilerParams(dimension_semantics=sem, vmem_limit_bytes=VMEM_LIMIT_BYTES)


def _row_tile(rows, target, mult=16):
    best = None
    for t in range(mult, min(rows, target) + 1, mult):
        if rows % t == 0:
            best = t
    assert best is not None, (rows, target)
    return best


def _dot(a, b):
    return jnp.dot(a, b, preferred_element_type=F32)


def _dot_nt(a, b):
    return lax.dot_general(a, b, (((1,), (1,)), ((), ())), preferred_element_type=F32)


def _dot3(x, w):
    xh = x.astype(BF16)
    xl = (x - xh.astype(F32)).astype(BF16)
    wh = w.astype(BF16)
    wl = (w - wh.astype(F32)).astype(BF16)
    return _dot(xh, wh) + (_dot(xh, wl) + _dot(xl, wh))


def _ln_rows(x, g, b):
    mu = jnp.mean(x, axis=-1, keepdims=True)
    xc = x - mu
    var = jnp.mean(xc * xc, axis=-1, keepdims=True)
    return xc * lax.rsqrt(var + LN_EPS) * g + b


def _ln_kernel(x_ref, g_ref, b_ref, y_ref, yb_ref):
    y = _ln_rows(x_ref[...], g_ref[...], b_ref[...])
    y_ref[...] = y
    yb_ref[...] = y.astype(BF16)


def _add_ln_kernel(alpha, h_ref, z_ref, g_ref, b_ref, y_ref, yb_ref):
    y = _ln_rows(alpha * h_ref[...] + z_ref[...], g_ref[...], b_ref[...])
    y_ref[...] = y
    yb_ref[...] = y.astype(BF16)


def _layer_norm(x, g, b, h=None, alpha=None):
    rows, d = x.shape
    tm = _row_tile(rows, 512)
    row_spec = pl.BlockSpec((tm, d), lambda i: (i, 0))
    vec_spec = pl.BlockSpec((1, d), lambda i: (0, 0))
    g2, b2 = g.reshape(1, d), b.reshape(1, d)
    out_shape = (jax.ShapeDtypeStruct((rows, d), F32), jax.ShapeDtypeStruct((rows, d), BF16))
    if h is None:
        return pl.pallas_call(
            _ln_kernel, grid=(rows // tm,), in_specs=[row_spec, vec_spec, vec_spec],
            out_specs=(row_spec, row_spec), out_shape=out_shape,
            compiler_params=_params(("parallel",)), name="ln_in")(x, g2, b2)
    return pl.pallas_call(
        functools.partial(_add_ln_kernel, alpha), grid=(rows // tm,),
        in_specs=[row_spec, row_spec, vec_spec, vec_spec],
        out_specs=(row_spec, row_spec), out_shape=out_shape,
        compiler_params=_params(("parallel",)), name="add_ln")(h, x, g2, b2)


def _proj_kernel(qscale, hb_ref, w_ref, cos_ref, sin_ref, f_ref, b_ref, wb_ref):
    j = pl.program_id(0)

    @pl.when(pl.program_id(1) == 0)
    def _():
        wb_ref[...] = w_ref[...].astype(BF16)

    acc = _dot(hb_ref[...], wb_ref[...])

    @pl.when(j <= SEG_AK)
    def _():
        cos = jnp.concatenate([cos_ref[...]] * N_HEADS, axis=1)
        sin = jnp.concatenate([sin_ref[...]] * N_HEADS, axis=1)
        lane = lax.broadcasted_iota(I32, acc.shape, 1)
        first_half = (lane % A_HEAD_DIM) < (A_HEAD_DIM // 2)
        partner = jnp.where(first_half,
                            pltpu.roll(acc, WIDTH - A_HEAD_DIM // 2, 1),
                            pltpu.roll(acc, A_HEAD_DIM // 2, 1))
        rot = acc * cos + partner * sin
        f_ref[...] = rot
        scale = jnp.where(j == SEG_AQ, qscale[SEG_AQ], 1.0)
        b_ref[...] = (rot * scale).astype(BF16)

    @pl.when(j > SEG_AK)
    def _():
        f_ref[...] = acc
        scale = jnp.where((j == SEG_BQ) | (j == SEG_CQ), qscale[SEG_BQ], 1.0)
        b_ref[...] = (acc * scale).astype(BF16)


def _project(hb, w_in, layer, cos_t, sin_t):
    rows, d = hb.shape
    tm = _row_tile(rows, 640)
    qscale = {SEG_AQ: A_HEAD_DIM ** -0.5, SEG_BQ: HEAD_W ** -0.5}
    return pl.pallas_call(
        functools.partial(_proj_kernel, qscale),
        grid=(N_SEG, rows // tm),
        in_specs=[
            pl.BlockSpec((tm, d), lambda j, i: (i, 0)),
            pl.BlockSpec((None, d, WIDTH), lambda j, i: (layer, 0, j)),
            pl.BlockSpec((tm, HEAD_W), lambda j, i: (i, 0)),
            pl.BlockSpec((tm, HEAD_W), lambda j, i: (i, 0)),
        ],
        out_specs=(pl.BlockSpec((None, tm, WIDTH), lambda j, i: (j, i, 0)),
                   pl.BlockSpec((None, tm, WIDTH), lambda j, i: (j, i, 0))),
        out_shape=(jax.ShapeDtypeStruct((N_SEG, rows, WIDTH), F32),
                   jax.ShapeDtypeStruct((N_SEG, rows, WIDTH), BF16)),
        scratch_shapes=[pltpu.VMEM((d, WIDTH), BF16)],
        compiler_params=_params(("arbitrary", "arbitrary")), name="in_proj")(hb, w_in, cos_t, sin_t)


def _forget_kernel(h_ref, w_ref, bias_ref, o_ref):
    x = _dot3(h_ref[...], w_ref[...]) + bias_ref[...]
    o_ref[...] = jnp.minimum(x, 0.0) - jnp.log(1.0 + jnp.exp(-jnp.abs(x)))


def _forget_logf(h, w_f, b_f):
    rows, d = h.shape
    tm = _row_tile(rows, 512)
    return pl.pallas_call(
        _forget_kernel, grid=(rows // tm,),
        in_specs=[pl.BlockSpec((tm, d), lambda i: (i, 0)),
                  pl.BlockSpec((d, HEAD_W), lambda i: (0, 0)),
                  pl.BlockSpec((1, HEAD_W), lambda i: (0, 0))],
        out_specs=pl.BlockSpec((tm, HEAD_W), lambda i: (i, 0)),
        out_shape=jax.ShapeDtypeStruct((rows, HEAD_W), F32),
        compiler_params=_params(("parallel",)), name="forget_gate")(h, w_f, b_f)


def _cumsum_kernel(x_ref, o_ref):
    x = x_ref[...]
    n = x.shape[1]
    lane = lax.broadcasted_iota(I32, x.shape, 1)
    shift = 1
    while shift < n:
        x = x + jnp.where(lane >= shift, pltpu.roll(x, shift, 1), 0.0)
        shift *= 2
    o_ref[...] = x


def _cumsum_lanes(x):
    return pl.pallas_call(
        _cumsum_kernel, out_shape=jax.ShapeDtypeStruct(x.shape, F32),
        compiler_params=pltpu.CompilerParams(vmem_limit_bytes=VMEM_LIMIT_BYTES), name="cumsum")(x)


def _online_update(s, v, m_ref, l_ref, acc_ref):
    m_old = m_ref[...]
    m_new = jnp.maximum(m_old, jnp.max(s, axis=-1, keepdims=True))
    a = jnp.exp(m_old - m_new)
    p = jnp.exp(s - m_new)
    l_ref[...] = a * l_ref[...] + jnp.sum(p, axis=-1, keepdims=True)
    acc_ref[...] = a * acc_ref[...] + _dot(p.astype(BF16), v)
    m_ref[...] = m_new


def _init_stats(m_ref, l_ref, acc_ref):
    m_ref[...] = jnp.full(m_ref.shape, NEG_INF, F32)
    l_ref[...] = jnp.zeros(l_ref.shape, F32)
    acc_ref[...] = jnp.zeros(acc_ref.shape, F32)


def _split_halves(q):
    lane = lax.broadcasted_iota(I32, q.shape, 1)
    zero = jnp.zeros_like(q)
    return jnp.concatenate([jnp.where(lane < A_HEAD_DIM, q, zero),
                            jnp.where(lane >= A_HEAD_DIM, q, zero)], axis=0)


def _diff_lambda(dl, lam_init):
    s1 = jnp.sum(dl[0:1, :] * dl[1:2, :], axis=-1, keepdims=True)
    s2 = jnp.sum(dl[2:3, :] * dl[3:4, :], axis=-1, keepdims=True)
    return jnp.exp(s1) - jnp.exp(s2) + lam_init


def _diff_finish(acc, l, n, lam, gain):
    o = acc[:n] / l[:n] - lam * (acc[n:] / l[n:])
    return o * lax.rsqrt(jnp.mean(o * o, axis=-1, keepdims=True) + LN_EPS) * gain


def _prompt_flash_kernel(kind, tq, tk, lam_init, *refs):
    if kind == "diff":
        q_ref, k_ref, v_ref, dl_ref, gain_ref, o_ref, qs_ref, m_ref, l_ref, acc_ref = refs
    else:
        q_ref, k_ref, v_ref, cq_ref, ck_ref, o_ref, qs_ref, m_ref, l_ref, acc_ref = refs
    h = pl.program_id(0)
    q0 = pl.program_id(1) * tq
    nq = qs_ref.shape[0]

    if kind == "diff":
        qs_ref[...] = _split_halves(q_ref[...])
    else:
        qs_ref[...] = q_ref[...]
        head_lane = lax.broadcasted_iota(I32, cq_ref.shape, 1) == h
        cq = jnp.sum(jnp.where(head_lane, cq_ref[...], 0.0), axis=-1, keepdims=True)
    _init_stats(m_ref, l_ref, acc_ref)

    def scores(kb):
        k0 = pl.multiple_of(kb * tk, tk)
        s = _dot_nt(qs_ref[...], k_ref[pl.ds(k0, tk), :])
        if kind == "forget":
            s = s + cq - ck_ref[:, pl.ds(k0, tk)]
        return s, v_ref[pl.ds(k0, tk), :]

    n_full = q0 // tk

    def body(kb, carry):
        s, v = scores(kb)
        _online_update(s, v, m_ref, l_ref, acc_ref)
        return carry

    lax.fori_loop(0, n_full, body, 0)

    s, v = scores(n_full)
    qpos = q0 + lax.broadcasted_iota(I32, (nq, tk), 0) % tq
    kpos = n_full * tk + lax.broadcasted_iota(I32, (nq, tk), 1)
    if kind == "diff":
        visible = (kpos // CHUNK) <= (qpos // CHUNK)
    else:
        visible = kpos <= qpos
    _online_update(jnp.where(visible, s, NEG_INF), v, m_ref, l_ref, acc_ref)

    if kind == "diff":
        lam = _diff_lambda(dl_ref[...], lam_init)
        o = _diff_finish(acc_ref[...], l_ref[...], tq, lam, gain_ref[...])
    else:
        o = acc_ref[...] / l_ref[...]
    o_ref[...] = o.astype(BF16)


def _prompt_flash(kind, qkv, seg_q, t, extra, lam_init=0.0):
    tq = _row_tile(t, 256 if kind == "diff" else 512, 128)
    tk = 2 * tq if kind == "diff" else tq
    tk = tk if t % tk == 0 else tq
    nq = 2 * tq if kind == "diff" else tq
    in_specs = [
        pl.BlockSpec((None, tq, HEAD_W), lambda h, i: (seg_q, i, h)),
        pl.BlockSpec((None, t, HEAD_W), lambda h, i: (seg_q + 1, 0, h)),
        pl.BlockSpec((None, t, HEAD_W), lambda h, i: (seg_q + 2, 0, h)),
    ]
    if kind == "diff":
        dl, gain = extra
        in_specs += [pl.BlockSpec(dl.shape, lambda h, i: (0, 0)),
                     pl.BlockSpec(gain.shape, lambda h, i: (0, 0))]
    else:
        cum, cum_t = extra
        in_specs += [pl.BlockSpec((tq, N_HEADS), lambda h, i: (i, 0)),
                     pl.BlockSpec((None, 1, t), lambda h, i: (h, 0, 0))]
    return pl.pallas_call(
        functools.partial(_prompt_flash_kernel, kind, tq, tk, lam_init),
        grid=(N_HEADS, t // tq), in_specs=in_specs,
        out_specs=pl.BlockSpec((tq, HEAD_W), lambda h, i: (i, h)),
        out_shape=jax.ShapeDtypeStruct((t, WIDTH), BF16),
        scratch_shapes=[pltpu.VMEM((nq, HEAD_W), BF16), pltpu.VMEM((nq, 1), F32),
                        pltpu.VMEM((nq, 1), F32), pltpu.VMEM((nq, HEAD_W), F32)],
        compiler_params=_params(("arbitrary", "arbitrary")),
        name="prompt_" + kind)(qkv, qkv, qkv, *extra)


BAND_TQ = 256
BAND_BLOCKS = BAND_CHUNKS * CHUNK // BAND_TQ + 1


def _prompt_band_kernel(q_ref, k_ref, v_ref, bias_ref, o_ref, m_ref, l_ref, acc_ref):
    i = pl.program_id(1)
    _init_stats(m_ref, l_ref, acc_ref)

    def body(j, carry):
        k0 = pl.multiple_of((i - (BAND_BLOCKS - 1) + j) * BAND_TQ, BAND_TQ)
        s = _dot_nt(q_ref[...], k_ref[pl.ds(k0, BAND_TQ), :]) + bias_ref[j]
        _online_update(s, v_ref[pl.ds(k0, BAND_TQ), :], m_ref, l_ref, acc_ref)
        return carry

    lax.fori_loop(jnp.maximum(BAND_BLOCKS - 1 - i, 0), BAND_BLOCKS, body, 0)
    o_ref[...] = (acc_ref[...] / l_ref[...]).astype(BF16)


def _band_bias_tiles(rel_bias_l):
    r = np.arange(BAND_TQ)[:, None]
    c = np.arange(BAND_TQ)[None, :]
    tiles = []
    for j in range(BAND_BLOCKS):
        off = (BAND_BLOCKS - 1 - j) * BAND_TQ
        rel = np.clip(off + r - c, -REL_CLIP, REL_CLIP) + REL_CLIP
        qc, kc = (off + r) // CHUNK, c // CHUNK
        visible = (kc <= qc) & (kc >= qc - BAND_CHUNKS)
        tiles.append(jnp.where(visible[None], rel_bias_l[:, rel], NEG_INF))
    return jnp.stack(tiles, axis=1).astype(F32)


def _prompt_band(qkv, t, bias_tiles):
    assert t % BAND_TQ == 0 and BAND_TQ % CHUNK == 0
    return pl.pallas_call(
        _prompt_band_kernel, grid=(N_HEADS, t // BAND_TQ),
        in_specs=[
            pl.BlockSpec((None, BAND_TQ, HEAD_W), lambda h, i: (SEG_BQ, i, h)),
            pl.BlockSpec((None, t, HEAD_W), lambda h, i: (SEG_BK, 0, h)),
            pl.BlockSpec((None, t, HEAD_W), lambda h, i: (SEG_BV, 0, h)),
            pl.BlockSpec((None, BAND_BLOCKS, BAND_TQ, BAND_TQ), lambda h, i: (h, 0, 0, 0)),
        ],
        out_specs=pl.BlockSpec((BAND_TQ, HEAD_W), lambda h, i: (i, h)),
        out_shape=jax.ShapeDtypeStruct((t, WIDTH), BF16),
        scratch_shapes=[pltpu.VMEM((BAND_TQ, 1), F32), pltpu.VMEM((BAND_TQ, 1), F32),
                        pltpu.VMEM((BAND_TQ, HEAD_W), F32)],
        compiler_params=_params(("arbitrary", "arbitrary")), name="prompt_band")(qkv, qkv, qkv, bias_tiles)


def _sample_attn_kernel(kind, n_kt, ts, lam_init, *refs):
    if kind == "diff":
        (q_ref, kc_ref, vc_ref, kn_ref, vn_ref, dl_ref, gain_ref, o_ref,
         qs_ref, m_ref, l_ref, acc_ref) = refs
    elif kind == "band":
        (q_ref, kc_ref, vc_ref, kn_ref, vn_ref, bc_ref, bn_ref, o_ref,
         qs_ref, m_ref, l_ref, acc_ref) = refs
    else:
        (q_ref, kc_ref, vc_ref, kn_ref, vn_ref, cq_ref, ckc_ref, ckn_ref, o_ref,
         qs_ref, m_ref, l_ref, acc_ref) = refs
    kt = pl.program_id(1)
    nq = qs_ref.shape[1]

    @pl.when(kt == 0)
    def _():
        for h in range(N_HEADS):
            q = q_ref[:, h * HEAD_W:(h + 1) * HEAD_W]
            qs_ref[h] = _split_halves(q) if kind == "diff" else q
        _init_stats(m_ref, l_ref, acc_ref)

    def head_bias(h, s, new):
        if kind == "band":
            return s + (bn_ref[h] if new else bc_ref[h])
        if kind == "forget":
            cq = cq_ref[:, h:h + 1]
            ck = ckn_ref[h:h + 1, :] if new else ckc_ref[h:h + 1, :]
            return s + cq - ck
        return s

    for h in range(N_HEADS):
        sl = slice(h * HEAD_W, (h + 1) * HEAD_W)
        s = _dot_nt(qs_ref[h], kc_ref[:, sl].astype(BF16))
        _online_update(head_bias(h, s, False), vc_ref[:, sl].astype(BF16),
                       m_ref.at[h], l_ref.at[h], acc_ref.at[h])

    @pl.when(kt == n_kt - 1)
    def _():
        for h in range(N_HEADS):
            sl = slice(h * HEAD_W, (h + 1) * HEAD_W)
            s = head_bias(h, _dot_nt(qs_ref[h], kn_ref[:, sl]), True)
            if kind == "forget":
                r = lax.broadcasted_iota(I32, (nq, ts), 0)
                c = lax.broadcasted_iota(I32, (nq, ts), 1)
                s = jnp.where(c <= r, s, NEG_INF)
            _online_update(s, vn_ref[:, sl], m_ref.at[h], l_ref.at[h], acc_ref.at[h])
            if kind == "diff":
                lam = _diff_lambda(dl_ref[...], lam_init)
                o = _diff_finish(acc_ref[h], l_ref[h], ts, lam, gain_ref[...])
            else:
                o = acc_ref[h] / l_ref[h]
            o_ref[:, sl] = o.astype(BF16)


def _sample_attn(kind, qkv, seg_q, cache_k, cache_v, layer, t_prompt, extra, lam_init=0.0):
    _, n_streams, past, _ = cache_k.shape
    rows = qkv.shape[1]
    ts = (rows - t_prompt) // n_streams
    assert t_prompt % ts == 0 and ts % 16 == 0
    tk = _row_tile(past, 1024, 128)
    n_kt = past // tk
    nq = 2 * ts if kind == "diff" else ts
    row0 = t_prompt // ts
    new_spec = lambda seg: pl.BlockSpec((None, ts, WIDTH), lambda b, kt: (seg, row0 + b, 0))
    cache_spec = pl.BlockSpec((None, None, tk, WIDTH), lambda b, kt: (layer, b, kt, 0))
    in_specs = [new_spec(seg_q), cache_spec, cache_spec, new_spec(seg_q + 1), new_spec(seg_q + 2)]
    if kind == "diff":
        dl, gain = extra
        in_specs += [pl.BlockSpec(dl.shape, lambda b, kt: (0, 0)),
                     pl.BlockSpec(gain.shape, lambda b, kt: (0, 0))]
    elif kind == "band":
        bias_c, bias_n = extra
        in_specs += [pl.BlockSpec((N_HEADS, ts, tk), lambda b, kt: (0, 0, kt)),
                     pl.BlockSpec(bias_n.shape, lambda b, kt: (0, 0, 0))]
    else:
        cq, ck_cache, ck_new = extra
        in_specs += [pl.BlockSpec((None, ts, N_HEADS), lambda b, kt: (b, 0, 0)),
                     pl.BlockSpec((None, N_HEADS, tk), lambda b, kt: (b, 0, kt)),
                     pl.BlockSpec((None, N_HEADS, ts), lambda b, kt: (b, 0, 0))]
    return pl.pallas_call(
        functools.partial(_sample_attn_kernel, kind, n_kt, ts, lam_init),
        grid=(n_streams, n_kt), in_specs=in_specs,
        out_specs=pl.BlockSpec((ts, WIDTH), lambda b, kt: (b, 0)),
        out_shape=jax.ShapeDtypeStruct((n_streams * ts, WIDTH), BF16),
        scratch_shapes=[pltpu.VMEM((N_HEADS, nq, HEAD_W), BF16), pltpu.VMEM((N_HEADS, nq, 1), F32),
                        pltpu.VMEM((N_HEADS, nq, 1), F32), pltpu.VMEM((N_HEADS, nq, HEAD_W), F32)],
        compiler_params=_params(("arbitrary", "arbitrary")),
        name="sample_" + kind)(qkv, cache_k, cache_v, qkv, qkv, *extra)


def _mix_kernel(hb_ref, oa_ref, ob_ref, oc_ref, wg0_ref, wg1_ref, wg2_ref, bg_ref,
                wa_ref, wb_ref, wc_ref, o_ref, wgs_ref, wbs_ref):
    @pl.when(pl.program_id(1) == 0)
    def _():
        for b, (wg, wbr) in enumerate(((wg0_ref, wa_ref), (wg1_ref, wb_ref), (wg2_ref, wc_ref))):
            wgs_ref[b] = wg[...].astype(BF16)
            wbs_ref[b] = wbr[...].astype(BF16)

    hb = hb_ref[...]
    mixed = None
    for b, o in enumerate((oa_ref, ob_ref, oc_ref)):
        gate = jax.nn.sigmoid(_dot(hb, wgs_ref[b]) + bg_ref[b:b + 1, :])
        term = gate * _dot(o[...], wbs_ref[b])
        mixed = term if mixed is None else mixed + term
    o_ref[...] = mixed.astype(BF16)


def _mix(hb, o_a, o_b, o_c, w_gate, b_gate_l, w_a, w_b, w_c, layer):
    rows, d = hb.shape
    tm = _row_tile(rows, 640)
    tn = 256
    nb = d // tn
    gate_spec = lambda b: pl.BlockSpec((None, d, tn), lambda n, i: (layer, 0, b * nb + n))
    branch_spec = pl.BlockSpec((None, WIDTH, tn), lambda n, i: (layer, 0, n))
    o_spec = pl.BlockSpec((tm, WIDTH), lambda n, i: (i, 0))
    return pl.pallas_call(
        _mix_kernel, grid=(nb, rows // tm),
        in_specs=[pl.BlockSpec((tm, d), lambda n, i: (i, 0)), o_spec, o_spec, o_spec,
                  gate_spec(0), gate_spec(1), gate_spec(2),
                  pl.BlockSpec((N_BRANCHES, tn), lambda n, i: (0, n)),
                  branch_spec, branch_spec, branch_spec],
        out_specs=pl.BlockSpec((tm, tn), lambda n, i: (i, n)),
        out_shape=jax.ShapeDtypeStruct((rows, d), BF16),
        scratch_shapes=[pltpu.VMEM((N_BRANCHES, d, tn), BF16), pltpu.VMEM((N_BRANCHES, WIDTH, tn), BF16)],
        compiler_params=_params(("arbitrary", "arbitrary")),
        name="branch_mix")(hb, o_a, o_b, o_c, w_gate, w_gate, w_gate, b_gate_l, w_a, w_b, w_c)


def _matmul_kernel(x_ref, w_ref, o_ref, wb_ref):
    @pl.when(pl.program_id(1) == 0)
    def _():
        wb_ref[...] = w_ref[...].astype(BF16)
    o_ref[...] = _dot(x_ref[...], wb_ref[...])


def _matmul(x, w, layer):
    rows, k = x.shape
    n = w.shape[-1]
    tm = _row_tile(rows, 640)
    tn = min(512, n)
    assert n % tn == 0
    return pl.pallas_call(
        _matmul_kernel, grid=(n // tn, rows // tm),
        in_specs=[pl.BlockSpec((tm, k), lambda j, i: (i, 0)),
                  pl.BlockSpec((None, k, tn), lambda j, i: (layer, 0, j))],
        out_specs=pl.BlockSpec((tm, tn), lambda j, i: (i, j)),
        out_shape=jax.ShapeDtypeStruct((rows, n), F32),
        scratch_shapes=[pltpu.VMEM((k, tn), BF16)],
        compiler_params=_params(("arbitrary", "arbitrary")), name="out_proj")(x, w)


def _router_kernel(h_ref, w_ref, b_ref, idx_ref, gate_ref, rank_ref, cnt_ref, run_ref):
    @pl.when(pl.program_id(0) == 0)
    def _():
        run_ref[...] = jnp.zeros(run_ref.shape, F32)

    logits = _dot3(h_ref[...], w_ref[...]) + b_ref[...]
    tm, n_e = logits.shape
    e_iota = lax.broadcasted_iota(I32, (tm, n_e), 1)
    out_lane = lax.broadcasted_iota(I32, (tm, HEAD_W), 1)
    vals, hots, idxs = [], [], []
    for _ in range(TOP_K):
        top = jnp.max(logits, axis=-1, keepdims=True)
        idx = jnp.min(jnp.where(logits == top, e_iota, n_e), axis=-1, keepdims=True)
        hot = e_iota == idx
        vals.append(top)
        idxs.append(idx)
        hots.append(hot)
        logits = jnp.where(hot, -jnp.inf, logits)
    exps = [jnp.exp(v - vals[0]) for v in vals]
    denom = exps[0] + exps[1] + exps[2] + exps[3]

    chosen = hots[0] | hots[1] | hots[2] | hots[3]
    chosen_f = jnp.where(chosen, 1.0, 0.0)
    r = lax.broadcasted_iota(I32, (tm, tm), 0)
    c = lax.broadcasted_iota(I32, (tm, tm), 1)
    before = jnp.where(c < r, 1.0, 0.0).astype(BF16)
    rank = _dot(before, chosen_f.astype(BF16)) + run_ref[...]
    run_ref[...] = run_ref[...] + jnp.sum(chosen_f, axis=0, keepdims=True)

    idx_out = jnp.zeros((tm, HEAD_W), I32)
    gate_out = jnp.zeros((tm, HEAD_W), F32)
    rank_out = jnp.zeros((tm, HEAD_W), I32)
    for k in range(TOP_K):
        rk = jnp.sum(jnp.where(hots[k], rank, 0.0), axis=-1, keepdims=True).astype(I32)
        idx_out = jnp.where(out_lane == k, idxs[k], idx_out)
        gate_out = jnp.where(out_lane == k, exps[k] / denom, gate_out)
        rank_out = jnp.where(out_lane == k, rk, rank_out)
    idx_ref[...] = idx_out
    gate_ref[...] = gate_out
    rank_ref[...] = rank_out
    cnt_ref[...] = run_ref[...].astype(I32)


def _router(h, w_router_l, b_router_l):
    rows, d = h.shape
    n_e = w_router_l.shape[-1]
    tm = _row_tile(rows, 256)
    lanes = pl.BlockSpec((tm, HEAD_W), lambda i: (i, 0))
    return pl.pallas_call(
        _router_kernel, grid=(rows // tm,),
        in_specs=[pl.BlockSpec((tm, d), lambda i: (i, 0)),
                  pl.BlockSpec((d, n_e), lambda i: (0, 0)),
                  pl.BlockSpec((1, n_e), lambda i: (0, 0))],
        out_specs=(lanes, lanes, lanes, pl.BlockSpec((1, n_e), lambda i: (0, 0))),
        out_shape=(jax.ShapeDtypeStruct((rows, HEAD_W), I32), jax.ShapeDtypeStruct((rows, HEAD_W), F32),
                   jax.ShapeDtypeStruct((rows, HEAD_W), I32), jax.ShapeDtypeStruct((1, n_e), I32)),
        scratch_shapes=[pltpu.VMEM((1, n_e), F32)],
        compiler_params=_params(("arbitrary",)), name="router")(h, w_router_l, b_router_l.reshape(1, n_e))


MOE_TM = 256


def _expert_changed(te_ref, i):
    return (i == 0) | (te_ref[i] != te_ref[jnp.maximum(i - 1, 0)])


def _gmm1_kernel(te_ref, na_ref, x_ref, wg_ref, wu_ref, bg_ref, bu_ref, o_ref, wgs_ref, wus_ref):
    i = pl.program_id(1)
    live = i < na_ref[0]

    @pl.when(live & _expert_changed(te_ref, i))
    def _():
        wgs_ref[...] = wg_ref[...].astype(BF16)
        wus_ref[...] = wu_ref[...].astype(BF16)

    @pl.when(live)
    def _():
        x = x_ref[...]
        g = jnp.minimum(_dot(x, wgs_ref[...]) + bg_ref[...], SWIGLU_LIMIT)
        u = jnp.clip(_dot(x, wus_ref[...]) + bu_ref[...], -SWIGLU_LIMIT, SWIGLU_LIMIT)
        o_ref[...] = (g * jax.nn.sigmoid(SWIGLU_ALPHA * g) * (u + 1.0)).astype(BF16)


def _gmm2_kernel(te_ref, na_ref, x_ref, w_ref, b_ref, o_ref, ws_ref):
    i = pl.program_id(1)
    live = i < na_ref[0]

    @pl.when(live & _expert_changed(te_ref, i))
    def _():
        ws_ref[...] = w_ref[...].astype(BF16)

    @pl.when(live)
    def _():
        o_ref[...] = _dot(x_ref[...], ws_ref[...]) + b_ref[...]


def _experts(xs, tile_expert, n_active, w_gate_up, b_gate_up_l, w_down, b_down_l, layer):
    n_rows, d = xs.shape
    ff = w_down.shape[2]
    n_tiles = n_rows // MOE_TM
    tn = min(1024, ff)
    nff = ff // tn
    tile = lambda i, na: jnp.minimum(i, na[0] - 1)
    act = pl.pallas_call(
        _gmm1_kernel,
        grid_spec=pltpu.PrefetchScalarGridSpec(
            num_scalar_prefetch=2, grid=(nff, n_tiles),
            in_specs=[
                pl.BlockSpec((MOE_TM, d), lambda n, i, te, na: (tile(i, na), 0)),
                pl.BlockSpec((None, None, d, tn), lambda n, i, te, na: (layer, te[i], 0, n)),
                pl.BlockSpec((None, None, d, tn), lambda n, i, te, na: (layer, te[i], 0, nff + n)),
                pl.BlockSpec((None, 1, tn), lambda n, i, te, na: (te[i], 0, n)),
                pl.BlockSpec((None, 1, tn), lambda n, i, te, na: (te[i], 0, nff + n)),
            ],
            out_specs=pl.BlockSpec((MOE_TM, tn), lambda n, i, te, na: (tile(i, na), n)),
            scratch_shapes=[pltpu.VMEM((d, tn), BF16), pltpu.VMEM((d, tn), BF16)]),
        out_shape=jax.ShapeDtypeStruct((n_rows, ff), BF16),
        compiler_params=_params(("arbitrary", "arbitrary")),
        name="expert_up")(tile_expert, n_active, xs, w_gate_up, w_gate_up, b_gate_up_l, b_gate_up_l)
    tn2 = min(1024, d)
    return pl.pallas_call(
        _gmm2_kernel,
        grid_spec=pltpu.PrefetchScalarGridSpec(
            num_scalar_prefetch=2, grid=(d // tn2, n_tiles),
            in_specs=[
                pl.BlockSpec((MOE_TM, ff), lambda n, i, te, na: (tile(i, na), 0)),
                pl.BlockSpec((None, None, ff, tn2), lambda n, i, te, na: (layer, te[i], 0, n)),
                pl.BlockSpec((None, 1, tn2), lambda n, i, te, na: (te[i], 0, n)),
            ],
            out_specs=pl.BlockSpec((MOE_TM, tn2), lambda n, i, te, na: (tile(i, na), n)),
            scratch_shapes=[pltpu.VMEM((ff, tn2), BF16)]),
        out_shape=jax.ShapeDtypeStruct((n_rows, d), F32),
        compiler_params=_params(("arbitrary", "arbitrary")),
        name="expert_down")(tile_expert, n_active, act, w_down, b_down_l)


def _moe(h, hb, w_router_l, b_router_l, w_gate_up, b_gate_up_l, w_down, b_down_l, layer):
    rows, d = h.shape
    n_e = w_router_l.shape[-1]
    idx, gate, rank, counts = _router(h, w_router_l, b_router_l)
    idx, gate, rank, counts = idx[:, :TOP_K], gate[:, :TOP_K], rank[:, :TOP_K], counts[0]
    padded = (counts + MOE_TM - 1) // MOE_TM * MOE_TM
    group_end = jnp.cumsum(padded)
    group_start = group_end - padded
    dest = group_start[idx] + rank
    n_tiles = (rows * TOP_K + n_e * (MOE_TM - 1)) // MOE_TM + 1
    n_rows = n_tiles * MOE_TM
    n_active = jnp.maximum(group_end[-1] // MOE_TM, 1).astype(I32)
    tile_ids = jnp.minimum(jnp.arange(n_tiles, dtype=I32), n_active - 1)
    tile_expert = jnp.minimum(jnp.searchsorted(group_end, tile_ids * MOE_TM, side="right"),
                              n_e - 1).astype(I32)
    tok = jnp.broadcast_to(jnp.arange(rows, dtype=I32)[:, None], (rows, TOP_K))
    row_tok = jnp.zeros((n_rows,), I32).at[dest.reshape(-1)].set(tok.reshape(-1))
    xs = hb[row_tok]
    y_rows = _experts(xs, tile_expert, n_active.reshape(1), w_gate_up,
                      b_gate_up_l.reshape(n_e, 1, -1), w_down, b_down_l.reshape(n_e, 1, -1), layer)
    return jnp.sum(y_rows[dest] * gate[:, :, None], axis=1)


def _rotary_tables(pos):
    half = A_HEAD_DIM // 2
    inv = ROPE_THETA ** (-jnp.arange(half, dtype=F32) / half)
    ang = pos.astype(F32)[:, None] * inv[None, :]
    cos, sin = jnp.cos(ang), jnp.sin(ang)
    reps = HEAD_W // A_HEAD_DIM
    return (jnp.concatenate([cos, cos] * reps, axis=1), jnp.concatenate([-sin, sin] * reps, axis=1))


def _sample_band_bias(rel_bias_l, q_pos, k_pos):
    rel = np.clip(q_pos[:, None] - k_pos[None, :], -REL_CLIP, REL_CLIP) + REL_CLIP
    qc, kc = q_pos[:, None] // CHUNK, k_pos[None, :] // CHUNK
    visible = (kc <= qc) & (kc >= qc - BAND_CHUNKS) & (k_pos[None, :] >= 0)
    return jnp.where(visible[None], rel_bias_l[:, rel], NEG_INF).astype(F32)


def kernel(x_prompt, x_sample, cache_a_k, cache_a_v, cache_b_k, cache_b_v, cache_c_k, cache_c_v, cache_c_logf, ln_in_g, ln_in_b, w_in, b_forget, diff_lambda, diff_subln_g, rel_bias, w_branch_a, w_branch_b, w_branch_c, w_gate, b_gate, w_out, ln_mix_g, ln_mix_b, w_router, b_router, w_gate_up, b_gate_up, w_down, b_down, ln_ffn_g, ln_ffn_b):
    n_p, t_p, d = x_prompt.shape
    n_s, t_s, _ = x_sample.shape
    depth, _, past = cache_a_k.shape[:3]
    band = cache_b_k.shape[2]
    assert n_p == 1 and past % CHUNK == 0 and t_s <= CHUNK and band == BAND_CHUNKS * CHUNK
    rows_s = n_s * t_s
    alpha = (2.0 * depth) ** 0.25

    x = jnp.concatenate([x_prompt.reshape(t_p, d), x_sample.reshape(rows_s, d)], axis=0)
    pos = jnp.concatenate([jnp.arange(t_p), jnp.tile(past + jnp.arange(t_s), n_s)])
    cos_t, sin_t = _rotary_tables(pos)
    flat = lambda c: c.reshape(c.shape[:3] + (WIDTH,))
    cache_a_k, cache_a_v, cache_b_k, cache_b_v, cache_c_k, cache_c_v = map(
        flat, (cache_a_k, cache_a_v, cache_b_k, cache_b_v, cache_c_k, cache_c_v))
    q_pos_s = past + np.arange(t_s)
    k_pos_band = past - band + np.arange(band + t_s)
    pad_t = (-(past + t_s)) % HEAD_W

    h, hb = _layer_norm(x, ln_in_g, ln_in_b)
    states = []
    for l in range(depth):
        lam_init = 0.8 - 0.6 * math.exp(-0.3 * l)
        gain = (diff_subln_g[l] * (1.0 - lam_init)).reshape(1, HEAD_W)
        dl = diff_lambda[l]

        seg_f, seg_b = _project(hb, w_in, l, cos_t, sin_t)
        w_f = jnp.pad(w_in[l, :, N_SEG * WIDTH:], ((0, 0), (0, HEAD_W - N_HEADS)))
        b_f = jnp.pad(b_forget[l], (0, HEAD_W - N_HEADS)).reshape(1, HEAD_W)
        logf = _forget_logf(h, w_f, b_f)[:, :N_HEADS]
        logf_p, logf_s = logf[:t_p], logf[t_p:].reshape(n_s, t_s, N_HEADS)

        cum_p_t = _cumsum_lanes(logf_p.T)
        o_a_p = _prompt_flash("diff", seg_b, SEG_AQ, t_p, (dl, gain), lam_init)
        o_b_p = _prompt_band(seg_b, t_p, _band_bias_tiles(rel_bias[l]))
        o_c_p = _prompt_flash("forget", seg_b, SEG_CQ, t_p,
                              (cum_p_t.T, cum_p_t.reshape(N_HEADS, 1, t_p)))

        logf_all = jnp.concatenate([cache_c_logf[l], logf_s], axis=1)
        logf_all = jnp.pad(jnp.swapaxes(logf_all, 1, 2), ((0, 0), (0, 0), (0, pad_t)))
        cum_s = _cumsum_lanes(logf_all.reshape(n_s * N_HEADS, -1)).reshape(n_s, N_HEADS, -1)
        cum_new = cum_s[:, :, past:past + t_s]
        bias_s = _sample_band_bias(rel_bias[l], q_pos_s, k_pos_band)
        o_a_s = _sample_attn("diff", seg_b, SEG_AQ, cache_a_k, cache_a_v, l, t_p, (dl, gain), lam_init)
        o_b_s = _sample_attn("band", seg_b, SEG_BQ, cache_b_k, cache_b_v, l, t_p,
                             (bias_s[:, :, :band], bias_s[:, :, band:]))
        o_c_s = _sample_attn("forget", seg_b, SEG_CQ, cache_c_k, cache_c_v, l, t_p,
                             (jnp.swapaxes(cum_new, 1, 2), cum_s[:, :, :past], cum_new))

        o_a = jnp.concatenate([o_a_p, o_a_s], axis=0)
        o_b = jnp.concatenate([o_b_p, o_b_s], axis=0)
        o_c = jnp.concatenate([o_c_p, o_c_s], axis=0)
        mixed = _mix(hb, o_a, o_b, o_c, w_gate, b_gate[l].reshape(N_BRANCHES, d),
                     w_branch_a, w_branch_b, w_branch_c, l)
        h, hb = _layer_norm(_matmul(mixed, w_out, l), ln_mix_g[l], ln_mix_b[l], h, alpha)
        moe = _moe(h, hb, w_router[l], b_router[l], w_gate_up, b_gate_up[l], w_down, b_down[l], l)
        h, hb = _layer_norm(moe, ln_ffn_g[l], ln_ffn_b[l], h, alpha)
        states.append((seg_f, logf_p, logf_s))

    def prompt_state(seg, keep=None):
        out = jnp.stack([st[0][seg, :t_p] for st in states]).reshape(depth, 1, t_p, N_HEADS, HEAD_W)
        return out if keep is None else out[:, :, -keep:]

    def sample_state(seg):
        return jnp.stack([st[0][seg, t_p:] for st in states]).reshape(depth, n_s, t_s, N_HEADS, HEAD_W)

    band_p = min(BAND_CHUNKS * CHUNK, t_p)
    return (h[:t_p].reshape(1, t_p, d), h[t_p:].reshape(n_s, t_s, d),
            prompt_state(SEG_AK), prompt_state(SEG_AV),
            prompt_state(SEG_BK, band_p), prompt_state(SEG_BV, band_p),
            prompt_state(SEG_CK), prompt_state(SEG_CV),
            jnp.stack([st[1] for st in states]).reshape(depth, 1, t_p, N_HEADS),
            sample_state(SEG_AK), sample_state(SEG_AV), sample_state(SEG_BK), sample_state(SEG_BV),
            sample_state(SEG_CK), sample_state(SEG_CV),
            jnp.stack([st[2] for st in states]))
```

```python
import functools
import math

import numpy as np
import jax
import jax.numpy as jnp
from jax import lax
from jax.experimental import pallas as pl
from jax.experimental.pallas import tpu as pltpu

F32 = jnp.float32
BF16 = jnp.bfloat16
I32 = jnp.int32

CHUNK = 64
N_HEADS = 8
HEAD_W = 128
WIDTH = N_HEADS * HEAD_W
A_HEAD_DIM = 64
BAND_CHUNKS = 8
REL_CLIP = 128
N_BRANCHES = 3
N_SEG = 9
N_EXPERTS = 32
TOP_K = 4
SWIGLU_LIMIT = 7.0
SWIGLU_ALPHA = 1.702
ROPE_THETA = 10000.0
LN_EPS = 1e-5
NEG_INF = -1e30
VMEM_LIMIT_BYTES = 56 * 1024 * 1024

SEG_AQ, SEG_AK, SEG_AV, SEG_BQ, SEG_BK, SEG_BV, SEG_CQ, SEG_CK, SEG_CV = range(N_SEG)


def _params(sem):
    return pltpu.CompilerParams(dimension_semantics=sem, vmem_limit_bytes=VMEM_LIMIT_BYTES)


def _row_tile(rows, target, mult=16):
    best = None
    for t in range(mult, min(rows, target) + 1, mult):
        if rows % t == 0:
            best = t
    assert best is not None, (rows, target)
    return best


def _dot(a, b):
    return jnp.dot(a, b, preferred_element_type=F32)


def _dot_nt(a, b):
    return lax.dot_general(a, b, (((1,), (1,)), ((), ())), preferred_element_type=F32)


def _dot3(x, w):
    xh = x.astype(BF16)
    xl = (x - xh.astype(F32)).astype(BF16)
    wh = w.astype(BF16)
    wl = (w - wh.astype(F32)).astype(BF16)
    return _dot(xh, wh) + (_dot(xh, wl) + _dot(xl, wh))


def _ln_rows(x, g, b):
    mu = jnp.mean(x, axis=-1, keepdims=True)
    xc = x - mu
    var = jnp.mean(xc * xc, axis=-1, keepdims=True)
    return xc * lax.rsqrt(var + LN_EPS) * g + b


def _ln_kernel(x_ref, g_ref, b_ref, y_ref, yb_ref):
    y = _ln_rows(x_ref[...], g_ref[...], b_ref[...])
    y_ref[...] = y
    yb_ref[...] = y.astype(BF16)


def _add_ln_kernel(alpha, h_ref, z_ref, g_ref, b_ref, y_ref, yb_ref):
    y = _ln_rows(alpha * h_ref[...] + z_ref[...], g_ref[...], b_ref[...])
    y_ref[...] = y
    yb_ref[...] = y.astype(BF16)


def _layer_norm(x, g, b, h=None, alpha=None):
    rows, d = x.shape
    tm = _row_tile(rows, 512)
    row_spec = pl.BlockSpec((tm, d), lambda i: (i, 0))
    vec_spec = pl.BlockSpec((1, d), lambda i: (0, 0))
    g2, b2 = g.reshape(1, d), b.reshape(1, d)
    out_shape = (jax.ShapeDtypeStruct((rows, d), F32), jax.ShapeDtypeStruct((rows, d), BF16))
    if h is None:
        return pl.pallas_call(
            _ln_kernel, grid=(rows // tm,), in_specs=[row_spec, vec_spec, vec_spec],
            out_specs=(row_spec, row_spec), out_shape=out_shape,
            compiler_params=_params(("parallel",)), name="ln_in")(x, g2, b2)
    return pl.pallas_call(
        functools.partial(_add_ln_kernel, alpha), grid=(rows // tm,),
        in_specs=[row_spec, row_spec, vec_spec, vec_spec],
        out_specs=(row_spec, row_spec), out_shape=out_shape,
        compiler_params=_params(("parallel",)), name="add_ln")(h, x, g2, b2)


def _proj_kernel(qscale, hb_ref, w_ref, cos_ref, sin_ref, f_ref, b_ref, wb_ref):
    j = pl.program_id(0)

    @pl.when(pl.program_id(1) == 0)
    def _():
        wb_ref[...] = w_ref[...].astype(BF16)

    acc = _dot(hb_ref[...], wb_ref[...])

    @pl.when(j <= SEG_AK)
    def _():
        cos = jnp.concatenate([cos_ref[...]] * N_HEADS, axis=1)
        sin = jnp.concatenate([sin_ref[...]] * N_HEADS, axis=1)
        lane = lax.broadcasted_iota(I32, acc.shape, 1)
        first_half = (lane % A_HEAD_DIM) < (A_HEAD_DIM // 2)
        partner = jnp.where(first_half,
                            pltpu.roll(acc, WIDTH - A_HEAD_DIM // 2, 1),
                            pltpu.roll(acc, A_HEAD_DIM // 2, 1))
        rot = acc * cos + partner * sin
        f_ref[...] = rot
        scale = jnp.where(j == SEG_AQ, qscale[SEG_AQ], 1.0)
        b_ref[...] = (rot * scale).astype(BF16)

    @pl.when(j > SEG_AK)
    def _():
        f_ref[...] = acc
        scale = jnp.where((j == SEG_BQ) | (j == SEG_CQ), qscale[SEG_BQ], 1.0)
        b_ref[...] = (acc * scale).astype(BF16)


def _project(hb, w_in, layer, cos_t, sin_t):
    rows, d = hb.shape
    tm = _row_tile(rows, 640)
    qscale = {SEG_AQ: A_HEAD_DIM ** -0.5, SEG_BQ: HEAD_W ** -0.5}
    return pl.pallas_call(
        functools.partial(_proj_kernel, qscale),
        grid=(N_SEG, rows // tm),
        in_specs=[
            pl.BlockSpec((tm, d), lambda j, i: (i, 0)),
            pl.BlockSpec((None, d, WIDTH), lambda j, i: (layer, 0, j)),
            pl.BlockSpec((tm, HEAD_W), lambda j, i: (i, 0)),
            pl.BlockSpec((tm, HEAD_W), lambda j, i: (i, 0)),
        ],
        out_specs=(pl.BlockSpec((None, tm, WIDTH), lambda j, i: (j, i, 0)),
                   pl.BlockSpec((None, tm, WIDTH), lambda j, i: (j, i, 0))),
        out_shape=(jax.ShapeDtypeStruct((N_SEG, rows, WIDTH), F32),
                   jax.ShapeDtypeStruct((N_SEG, rows, WIDTH), BF16)),
        scratch_shapes=[pltpu.VMEM((d, WIDTH), BF16)],
        compiler_params=_params(("arbitrary", "arbitrary")), name="in_proj")(hb, w_in, cos_t, sin_t)


def _forget_kernel(h_ref, w_ref, bias_ref, o_ref):
    x = _dot3(h_ref[...], w_ref[...]) + bias_ref[...]
    o_ref[...] = jnp.minimum(x, 0.0) - jnp.log(1.0 + jnp.exp(-jnp.abs(x)))


def _forget_logf(h, w_f, b_f):
    rows, d = h.shape
    tm = _row_tile(rows, 512)
    return pl.pallas_call(
        _forget_kernel, grid=(rows // tm,),
        in_specs=[pl.BlockSpec((tm, d), lambda i: (i, 0)),
                  pl.BlockSpec((d, HEAD_W), lambda i: (0, 0)),
                  pl.BlockSpec((1, HEAD_W), lambda i: (0, 0))],
        out_specs=pl.BlockSpec((tm, HEAD_W), lambda i: (i, 0)),
        out_shape=jax.ShapeDtypeStruct((rows, HEAD_W), F32),
        compiler_params=_params(("parallel",)), name="forget_gate")(h, w_f, b_f)


def _cumsum_kernel(x_ref, o_ref):
    x = x_ref[...]
    n = x.shape[1]
    lane = lax.broadcasted_iota(I32, x.shape, 1)
    shift = 1
    while shift < n:
        x = x + jnp.where(lane >= shift, pltpu.roll(x, shift, 1), 0.0)
        shift *= 2
    o_ref[...] = x


def _cumsum_lanes(x):
    return pl.pallas_call(
        _cumsum_kernel, out_shape=jax.ShapeDtypeStruct(x.shape, F32),
        compiler_params=pltpu.CompilerParams(vmem_limit_bytes=VMEM_LIMIT_BYTES), name="cumsum")(x)


def _lanes(x, n):
    if n <= HEAD_W:
        return x[:, :n]
    return jnp.tile(x, (1, n // HEAD_W))


def _with_ones(v):
    return jnp.concatenate([v, jnp.ones_like(v)], axis=1)


def _softmax_block(s, v1, m_ref, acc_ref):
    m_old = m_ref[...]
    m_new = jnp.maximum(m_old, jnp.max(s, axis=-1, keepdims=True))
    p = jnp.exp(s - _lanes(m_new, s.shape[1]))
    a = jnp.exp(m_old - m_new)
    acc_ref[...] = _lanes(a, 2 * HEAD_W) * acc_ref[...] + _dot(p.astype(BF16), v1)
    m_ref[...] = m_new


def _init_stats(m_ref, acc_ref):
    m_ref[...] = jnp.full(m_ref.shape, NEG_INF, F32)
    acc_ref[...] = jnp.zeros(acc_ref.shape, F32)


def _normalized(acc):
    return acc[:, :HEAD_W] / acc[:, HEAD_W:]


def _split_halves(q):
    lane = lax.broadcasted_iota(I32, q.shape, 1)
    zero = jnp.zeros_like(q)
    return jnp.concatenate([jnp.where(lane < A_HEAD_DIM, q, zero),
                            jnp.where(lane >= A_HEAD_DIM, q, zero)], axis=0)


def _diff_lambda(dl, lam_init):
    s1 = jnp.sum(dl[0:1, :] * dl[1:2, :], axis=-1, keepdims=True)
    s2 = jnp.sum(dl[2:3, :] * dl[3:4, :], axis=-1, keepdims=True)
    return jnp.exp(s1) - jnp.exp(s2) + lam_init


def _diff_finish(acc, n, lam, gain):
    o = _normalized(acc[:n]) - lam * _normalized(acc[n:])
    return o * lax.rsqrt(jnp.mean(o * o, axis=-1, keepdims=True) + LN_EPS) * gain


FLASH_SMALL = 512
FLASH_BIG = 2048


def _prompt_flash_kernel(kind, tq, big, lam_init, *refs):
    if kind == "diff":
        q_ref, k_ref, v_ref, dl_ref, gain_ref, o_ref, qs_ref, v1_ref, m_ref, acc_ref = refs
    else:
        q_ref, k_ref, v_ref, cq_ref, ck_ref, o_ref, qs_ref, v1_ref, cqr_ref, m_ref, acc_ref = refs
    h = pl.program_id(0)
    i = pl.program_id(1)
    q0 = i * tq
    nq = qs_ref.shape[0]

    @pl.when(i == 0)
    def _():
        v1_ref[...] = _with_ones(v_ref[...])

    if kind == "diff":
        qs_ref[...] = _split_halves(q_ref[...])
    else:
        qs_ref[...] = q_ref[...]
        head_lane = lax.broadcasted_iota(I32, cq_ref.shape, 1) == h
        cq = jnp.sum(jnp.where(head_lane, cq_ref[...], 0.0), axis=-1, keepdims=True)
        cqr_ref[...] = jnp.broadcast_to(cq, cqr_ref.shape)
    _init_stats(m_ref, acc_ref)

    def block(k0, tk, diagonal):
        s = _dot_nt(qs_ref[...], k_ref[pl.ds(k0, tk), :])
        if kind == "forget":
            s = s + _lanes(cqr_ref[...], tk) - ck_ref[:, pl.ds(k0, tk)]
        if diagonal:
            qpos = q0 + lax.broadcasted_iota(I32, (nq, tk), 0) % tq
            kpos = k0 + lax.broadcasted_iota(I32, (nq, tk), 1)
            visible = (kpos // CHUNK) <= (qpos // CHUNK) if kind == "diff" else kpos <= qpos
            s = jnp.where(visible, s, NEG_INF)
        _softmax_block(s, v1_ref[pl.ds(k0, tk), :], m_ref, acc_ref)

    def big_body(kb, carry):
        block(pl.multiple_of(kb * big, big), big, False)
        return carry

    def small_body(kb, carry):
        block(pl.multiple_of(kb * FLASH_SMALL, FLASH_SMALL), FLASH_SMALL, False)
        return carry

    n_big = q0 // big
    n_small = q0 // FLASH_SMALL
    lax.fori_loop(0, n_big, big_body, 0)
    lax.fori_loop(n_big * (big // FLASH_SMALL), n_small, small_body, 0)
    block(pl.multiple_of(n_small * FLASH_SMALL, FLASH_SMALL), FLASH_SMALL, True)

    if kind == "diff":
        lam = _diff_lambda(dl_ref[...], lam_init)
        o = _diff_finish(acc_ref[...], tq, lam, gain_ref[...])
    else:
        o = _normalized(acc_ref[...])
    o_ref[...] = o.astype(BF16)


def _prompt_flash(kind, qkv, seg_q, t, extra, lam_init=0.0):
    tq = 256 if kind == "diff" else 512
    big = min(FLASH_BIG, t)
    assert t % big == 0 and big % FLASH_SMALL == 0 and FLASH_SMALL % tq == 0
    nq = 2 * tq if kind == "diff" else tq
    in_specs = [
        pl.BlockSpec((None, tq, HEAD_W), lambda h, i: (seg_q, i, h)),
        pl.BlockSpec((None, t, HEAD_W), lambda h, i: (seg_q + 1, 0, h)),
        pl.BlockSpec((None, t, HEAD_W), lambda h, i: (seg_q + 2, 0, h)),
    ]
    scratch = [pltpu.VMEM((nq, HEAD_W), BF16), pltpu.VMEM((t, 2 * HEAD_W), BF16)]
    if kind == "diff":
        dl, gain = extra
        in_specs += [pl.BlockSpec(dl.shape, lambda h, i: (0, 0)),
                     pl.BlockSpec(gain.shape, lambda h, i: (0, 0))]
    else:
        cum, cum_t = extra
        in_specs += [pl.BlockSpec((tq, N_HEADS), lambda h, i: (i, 0)),
                     pl.BlockSpec((None, 1, t), lambda h, i: (h, 0, 0))]
        scratch += [pltpu.VMEM((tq, HEAD_W), F32)]
    scratch += [pltpu.VMEM((nq, HEAD_W), F32), pltpu.VMEM((nq, 2 * HEAD_W), F32)]
    return pl.pallas_call(
        functools.partial(_prompt_flash_kernel, kind, tq, big, lam_init),
        grid=(N_HEADS, t // tq), in_specs=in_specs,
        out_specs=pl.BlockSpec((tq, HEAD_W), lambda h, i: (i, h)),
        out_shape=jax.ShapeDtypeStruct((t, WIDTH), BF16),
        scratch_shapes=scratch,
        compiler_params=_params(("arbitrary", "arbitrary")),
        name="prompt_" + kind)(qkv, qkv, qkv, *extra)


BAND_TQ = 256
BAND_PAD = BAND_CHUNKS * CHUNK
BAND_WIN = BAND_PAD + BAND_TQ


def _prompt_band_kernel(q_ref, k_ref, v_ref, bias_ref, o_ref, kp_ref, v1_ref):
    i = pl.program_id(1)

    @pl.when(i == 0)
    def _():
        kp_ref[:BAND_PAD, :] = jnp.zeros((BAND_PAD, HEAD_W), BF16)
        kp_ref[BAND_PAD:, :] = k_ref[...]
        v1_ref[:BAND_PAD, :] = jnp.zeros((BAND_PAD, 2 * HEAD_W), BF16)
        v1_ref[BAND_PAD:, :] = _with_ones(v_ref[...])

    r0 = pl.multiple_of(i * BAND_TQ, BAND_TQ)
    s = _dot_nt(q_ref[...], kp_ref[pl.ds(r0, BAND_WIN), :]) + bias_ref[...]
    kpos = r0 - BAND_PAD + lax.broadcasted_iota(I32, s.shape, 1)
    s = jnp.where(kpos >= 0, s, NEG_INF)
    m = jnp.broadcast_to(jnp.max(s, axis=-1, keepdims=True), (BAND_TQ, HEAD_W))
    p = jnp.exp(s - _lanes(m, BAND_WIN))
    o_ref[...] = _normalized(_dot(p.astype(BF16), v1_ref[pl.ds(r0, BAND_WIN), :])).astype(BF16)


def _band_bias_window(rel_bias_l):
    n_tab = rel_bias_l.shape[0]
    span = BAND_TQ + BAND_WIN - 1
    dist = BAND_PAD + (BAND_TQ - 1) - np.arange(span)
    e = rel_bias_l[:, np.clip(dist, -REL_CLIP, REL_CLIP) + REL_CLIP]
    period = span + 1
    f = jnp.concatenate([e[:, BAND_TQ - 1:], jnp.zeros((n_tab, 1), e.dtype), e[:, :BAND_TQ - 1]], axis=1)
    skew = jnp.tile(f, (1, BAND_TQ))[:, :BAND_TQ * span].reshape(n_tab, BAND_TQ, span)
    bias = skew[:, :, :BAND_WIN]
    r = np.arange(BAND_TQ)[:, None]
    c = np.arange(BAND_WIN)[None, :] - BAND_PAD
    visible = (c // CHUNK <= r // CHUNK) & (c // CHUNK >= r // CHUNK - BAND_CHUNKS)
    assert period == span + 1
    return jnp.where(visible[None], bias, NEG_INF).astype(F32)


def _prompt_band(qkv, t, bias):
    assert t % BAND_TQ == 0 and BAND_TQ % CHUNK == 0
    return pl.pallas_call(
        _prompt_band_kernel, grid=(N_HEADS, t // BAND_TQ),
        in_specs=[
            pl.BlockSpec((None, BAND_TQ, HEAD_W), lambda h, i: (SEG_BQ, i, h)),
            pl.BlockSpec((None, t, HEAD_W), lambda h, i: (SEG_BK, 0, h)),
            pl.BlockSpec((None, t, HEAD_W), lambda h, i: (SEG_BV, 0, h)),
            pl.BlockSpec((None, BAND_TQ, BAND_WIN), lambda h, i: (h, 0, 0)),
        ],
        out_specs=pl.BlockSpec((BAND_TQ, HEAD_W), lambda h, i: (i, h)),
        out_shape=jax.ShapeDtypeStruct((t, WIDTH), BF16),
        scratch_shapes=[pltpu.VMEM((t + BAND_PAD, HEAD_W), BF16), pltpu.VMEM((t + BAND_PAD, 2 * HEAD_W), BF16)],
        compiler_params=_params(("arbitrary", "arbitrary")), name="prompt_band")(qkv, qkv, qkv, bias)


def _sample_attn_kernel(kind, n_kt, ts, lam_init, *refs):
    if kind == "diff":
        (q_ref, kc_ref, vc_ref, kn_ref, vn_ref, dl_ref, gain_ref, o_ref, qs_ref, m_ref, acc_ref) = refs
    elif kind == "band":
        (q_ref, kc_ref, vc_ref, kn_ref, vn_ref, bc_ref, bn_ref, o_ref, qs_ref, m_ref, acc_ref) = refs
    else:
        (q_ref, kc_ref, vc_ref, kn_ref, vn_ref, cq_ref, ckc_ref, ckn_ref, o_ref, qs_ref, m_ref, acc_ref) = refs
    kt = pl.program_id(1)
    nq = qs_ref.shape[1]

    @pl.when(kt == 0)
    def _():
        for h in range(N_HEADS):
            q = q_ref[:, h * HEAD_W:(h + 1) * HEAD_W]
            qs_ref[h] = _split_halves(q) if kind == "diff" else q
        _init_stats(m_ref, acc_ref)

    def head_bias(h, s, new):
        if kind == "band":
            return s + (bn_ref[h] if new else bc_ref[h])
        if kind == "forget":
            cq = cq_ref[:, h:h + 1]
            ck = ckn_ref[h:h + 1, :] if new else ckc_ref[h:h + 1, :]
            return s + cq - ck
        return s

    for h in range(N_HEADS):
        sl = slice(h * HEAD_W, (h + 1) * HEAD_W)
        s = _dot_nt(qs_ref[h], kc_ref[:, sl].astype(BF16))
        _softmax_block(head_bias(h, s, False), _with_ones(vc_ref[:, sl].astype(BF16)),
                       m_ref.at[h], acc_ref.at[h])

    @pl.when(kt == n_kt - 1)
    def _():
        for h in range(N_HEADS):
            sl = slice(h * HEAD_W, (h + 1) * HEAD_W)
            s = head_bias(h, _dot_nt(qs_ref[h], kn_ref[:, sl]), True)
            if kind == "forget":
                r = lax.broadcasted_iota(I32, (nq, ts), 0)
                c = lax.broadcasted_iota(I32, (nq, ts), 1)
                s = jnp.where(c <= r, s, NEG_INF)
            _softmax_block(s, _with_ones(vn_ref[:, sl]), m_ref.at[h], acc_ref.at[h])
            if kind == "diff":
                lam = _diff_lambda(dl_ref[...], lam_init)
                o = _diff_finish(acc_ref[h], ts, lam, gain_ref[...])
            else:
                o = _normalized(acc_ref[h])
            o_ref[:, sl] = o.astype(BF16)


def _sample_attn(kind, qkv, seg_q, cache_k, cache_v, layer, t_prompt, extra, lam_init=0.0):
    _, n_streams, past, _ = cache_k.shape
    rows = qkv.shape[1]
    ts = (rows - t_prompt) // n_streams
    assert t_prompt % ts == 0 and ts % 16 == 0
    tk = _row_tile(past, 1024, 128)
    n_kt = past // tk
    nq = 2 * ts if kind == "diff" else ts
    row0 = t_prompt // ts
    new_spec = lambda seg: pl.BlockSpec((None, ts, WIDTH), lambda b, kt: (seg, row0 + b, 0))
    cache_spec = pl.BlockSpec((None, None, tk, WIDTH), lambda b, kt: (layer, b, kt, 0))
    in_specs = [new_spec(seg_q), cache_spec, cache_spec, new_spec(seg_q + 1), new_spec(seg_q + 2)]
    if kind == "diff":
        dl, gain = extra
        in_specs += [pl.BlockSpec(dl.shape, lambda b, kt: (0, 0)),
                     pl.BlockSpec(gain.shape, lambda b, kt: (0, 0))]
    elif kind == "band":
        bias_c, bias_n = extra
        in_specs += [pl.BlockSpec((N_HEADS, ts, tk), lambda b, kt: (0, 0, kt)),
                     pl.BlockSpec(bias_n.shape, lambda b, kt: (0, 0, 0))]
    else:
        cq, ck_cache, ck_new = extra
        in_specs += [pl.BlockSpec((None, ts, N_HEADS), lambda b, kt: (b, 0, 0)),
                     pl.BlockSpec((None, N_HEADS, tk), lambda b, kt: (b, 0, kt)),
                     pl.BlockSpec((None, N_HEADS, ts), lambda b, kt: (b, 0, 0))]
    return pl.pallas_call(
        functools.partial(_sample_attn_kernel, kind, n_kt, ts, lam_init),
        grid=(n_streams, n_kt), in_specs=in_specs,
        out_specs=pl.BlockSpec((ts, WIDTH), lambda b, kt: (b, 0)),
        out_shape=jax.ShapeDtypeStruct((n_streams * ts, WIDTH), BF16),
        scratch_shapes=[pltpu.VMEM((N_HEADS, nq, HEAD_W), BF16), pltpu.VMEM((N_HEADS, nq, HEAD_W), F32),
                        pltpu.VMEM((N_HEADS, nq, 2 * HEAD_W), F32)],
        compiler_params=_params(("arbitrary", "arbitrary")),
        name="sample_" + kind)(qkv, cache_k, cache_v, qkv, qkv, *extra)


def _mix_kernel(hb_ref, oa_ref, ob_ref, oc_ref, wg0_ref, wg1_ref, wg2_ref, bg_ref,
                wa_ref, wb_ref, wc_ref, o_ref, wgs_ref, wbs_ref):
    @pl.when(pl.program_id(1) == 0)
    def _():
        for b, (wg, wbr) in enumerate(((wg0_ref, wa_ref), (wg1_ref, wb_ref), (wg2_ref, wc_ref))):
            wgs_ref[b] = wg[...].astype(BF16)
            wbs_ref[b] = wbr[...].astype(BF16)

    hb = hb_ref[...]
    mixed = None
    for b, o in enumerate((oa_ref, ob_ref, oc_ref)):
        gate = jax.nn.sigmoid(_dot(hb, wgs_ref[b]) + bg_ref[b:b + 1, :])
        term = gate * _dot(o[...], wbs_ref[b])
        mixed = term if mixed is None else mixed + term
    o_ref[...] = mixed.astype(BF16)


def _mix(hb, o_a, o_b, o_c, w_gate, b_gate_l, w_a, w_b, w_c, layer):
    rows, d = hb.shape
    tm = _row_tile(rows, 640)
    tn = 256
    nb = d // tn
    gate_spec = lambda b: pl.BlockSpec((None, d, tn), lambda n, i: (layer, 0, b * nb + n))
    branch_spec = pl.BlockSpec((None, WIDTH, tn), lambda n, i: (layer, 0, n))
    o_spec = pl.BlockSpec((tm, WIDTH), lambda n, i: (i, 0))
    return pl.pallas_call(
        _mix_kernel, grid=(nb, rows // tm),
        in_specs=[pl.BlockSpec((tm, d), lambda n, i: (i, 0)), o_spec, o_spec, o_spec,
                  gate_spec(0), gate_spec(1), gate_spec(2),
                  pl.BlockSpec((N_BRANCHES, tn), lambda n, i: (0, n)),
                  branch_spec, branch_spec, branch_spec],
        out_specs=pl.BlockSpec((tm, tn), lambda n, i: (i, n)),
        out_shape=jax.ShapeDtypeStruct((rows, d), BF16),
        scratch_shapes=[pltpu.VMEM((N_BRANCHES, d, tn), BF16), pltpu.VMEM((N_BRANCHES, WIDTH, tn), BF16)],
        compiler_params=_params(("arbitrary", "arbitrary")),
        name="branch_mix")(hb, o_a, o_b, o_c, w_gate, w_gate, w_gate, b_gate_l, w_a, w_b, w_c)


def _matmul_kernel(x_ref, w_ref, o_ref, wb_ref):
    @pl.when(pl.program_id(1) == 0)
    def _():
        wb_ref[...] = w_ref[...].astype(BF16)
    o_ref[...] = _dot(x_ref[...], wb_ref[...])


def _matmul(x, w, layer):
    rows, k = x.shape
    n = w.shape[-1]
    tm = _row_tile(rows, 640)
    tn = min(512, n)
    assert n % tn == 0
    return pl.pallas_call(
        _matmul_kernel, grid=(n // tn, rows // tm),
        in_specs=[pl.BlockSpec((tm, k), lambda j, i: (i, 0)),
                  pl.BlockSpec((None, k, tn), lambda j, i: (layer, 0, j))],
        out_specs=pl.BlockSpec((tm, tn), lambda j, i: (i, j)),
        out_shape=jax.ShapeDtypeStruct((rows, n), F32),
        scratch_shapes=[pltpu.VMEM((k, tn), BF16)],
        compiler_params=_params(("arbitrary", "arbitrary")), name="out_proj")(x, w)


def _router_kernel(h_ref, w_ref, b_ref, idx_ref, gate_ref, rank_ref, cnt_ref, run_ref):
    @pl.when(pl.program_id(0) == 0)
    def _():
        run_ref[...] = jnp.zeros(run_ref.shape, F32)

    logits = _dot3(h_ref[...], w_ref[...]) + b_ref[...]
    tm, n_e = logits.shape
    e_iota = lax.broadcasted_iota(I32, (tm, n_e), 1)
    out_lane = lax.broadcasted_iota(I32, (tm, HEAD_W), 1)
    vals, hots, idxs = [], [], []
    for _ in range(TOP_K):
        top = jnp.max(logits, axis=-1, keepdims=True)
        idx = jnp.min(jnp.where(logits == top, e_iota, n_e), axis=-1, keepdims=True)
        hot = e_iota == idx
        vals.append(top)
        idxs.append(idx)
        hots.append(hot)
        logits = jnp.where(hot, -jnp.inf, logits)
    exps = [jnp.exp(v - vals[0]) for v in vals]
    denom = exps[0] + exps[1] + exps[2] + exps[3]

    chosen = hots[0] | hots[1] | hots[2] | hots[3]
    chosen_f = jnp.where(chosen, 1.0, 0.0)
    r = lax.broadcasted_iota(I32, (tm, tm), 0)
    c = lax.broadcasted_iota(I32, (tm, tm), 1)
    before = jnp.where(c < r, 1.0, 0.0).astype(BF16)
    rank = _dot(before, chosen_f.astype(BF16)) + run_ref[...]
    run_ref[...] = run_ref[...] + jnp.sum(chosen_f, axis=0, keepdims=True)

    idx_out = jnp.zeros((tm, HEAD_W), I32)
    gate_out = jnp.zeros((tm, HEAD_W), F32)
    rank_out = jnp.zeros((tm, HEAD_W), I32)
    for k in range(TOP_K):
        rk = jnp.sum(jnp.where(hots[k], rank, 0.0), axis=-1, keepdims=True).astype(I32)
        idx_out = jnp.where(out_lane == k, idxs[k], idx_out)
        gate_out = jnp.where(out_lane == k, exps[k] / denom, gate_out)
        rank_out = jnp.where(out_lane == k, rk, rank_out)
    idx_ref[...] = idx_out
    gate_ref[...] = gate_out
    rank_ref[...] = rank_out
    cnt_ref[...] = run_ref[...].astype(I32)


def _router(h, w_router_l, b_router_l):
    rows, d = h.shape
    n_e = w_router_l.shape[-1]
    tm = _row_tile(rows, 256)
    lanes = pl.BlockSpec((tm, HEAD_W), lambda i: (i, 0))
    return pl.pallas_call(
        _router_kernel, grid=(rows // tm,),
        in_specs=[pl.BlockSpec((tm, d), lambda i: (i, 0)),
                  pl.BlockSpec((d, n_e), lambda i: (0, 0)),
                  pl.BlockSpec((1, n_e), lambda i: (0, 0))],
        out_specs=(lanes, lanes, lanes, pl.BlockSpec((1, n_e), lambda i: (0, 0))),
        out_shape=(jax.ShapeDtypeStruct((rows, HEAD_W), I32), jax.ShapeDtypeStruct((rows, HEAD_W), F32),
                   jax.ShapeDtypeStruct((rows, HEAD_W), I32), jax.ShapeDtypeStruct((1, n_e), I32)),
        scratch_shapes=[pltpu.VMEM((1, n_e), F32)],
        compiler_params=_params(("arbitrary",)), name="router")(h, w_router_l, b_router_l.reshape(1, n_e))


MOE_TM = 256
COMBINE_TM = 128


def _expert_changed(te_ref, i):
    return (i == 0) | (te_ref[i] != te_ref[jnp.maximum(i - 1, 0)])


def _gather_rows(src_hbm, idx_ref, base, n_rows, dst, sem):
    for r in range(n_rows):
        pltpu.make_async_copy(src_hbm.at[pl.ds(idx_ref[base + r], 1)], dst.at[pl.ds(r, 1)], sem).start()


def _wait_rows(src_hbm, n_rows, dst, sem):
    pltpu.make_async_copy(src_hbm.at[pl.ds(0, n_rows)], dst, sem).wait()


def _gmm1_kernel(te_ref, na_ref, tok_ref, h_hbm, wg_ref, wu_ref, bg_ref, bu_ref, o_ref,
                 wgs_ref, wus_ref, xbuf_ref, sem_ref):
    i = pl.program_id(1)
    n_active = na_ref[0]
    live = i < n_active
    slot = i % 2

    @pl.when(i == 0)
    def _():
        _gather_rows(h_hbm, tok_ref, 0, MOE_TM, xbuf_ref.at[0], sem_ref.at[0])

    @pl.when(i + 1 < n_active)
    def _():
        _gather_rows(h_hbm, tok_ref, (i + 1) * MOE_TM, MOE_TM, xbuf_ref.at[1 - slot], sem_ref.at[1 - slot])

    @pl.when(live & _expert_changed(te_ref, i))
    def _():
        wgs_ref[...] = wg_ref[...].astype(BF16)
        wus_ref[...] = wu_ref[...].astype(BF16)

    @pl.when(live)
    def _():
        _wait_rows(h_hbm, MOE_TM, xbuf_ref.at[slot], sem_ref.at[slot])
        x = xbuf_ref[slot].astype(BF16)
        g = jnp.minimum(_dot(x, wgs_ref[...]) + bg_ref[...], SWIGLU_LIMIT)
        u = jnp.clip(_dot(x, wus_ref[...]) + bu_ref[...], -SWIGLU_LIMIT, SWIGLU_LIMIT)
        o_ref[...] = (g * jax.nn.sigmoid(SWIGLU_ALPHA * g) * (u + 1.0)).astype(BF16)

    @pl.when(jnp.logical_not(live))
    def _():
        o_ref[...] = jnp.zeros(o_ref.shape, o_ref.dtype)


def _gmm2_kernel(te_ref, na_ref, x_ref, w_ref, b_ref, o_ref, ws_ref):
    i = pl.program_id(1)
    live = i < na_ref[0]

    @pl.when(live & _expert_changed(te_ref, i))
    def _():
        ws_ref[...] = w_ref[...].astype(BF16)

    @pl.when(live)
    def _():
        o_ref[...] = _dot(x_ref[...], ws_ref[...]) + b_ref[...]

    @pl.when(jnp.logical_not(live))
    def _():
        o_ref[...] = jnp.zeros(o_ref.shape, o_ref.dtype)


def _experts(h, row_tok, tile_expert, n_active, w_gate_up, b_gate_up_l, w_down, b_down_l, layer):
    n_rows = row_tok.shape[0]
    d = h.shape[1]
    ff = w_down.shape[2]
    n_tiles = n_rows // MOE_TM
    tn = min(1024, ff)
    nff = ff // tn
    tile = lambda i, na: jnp.minimum(i, na[0] - 1)
    act = pl.pallas_call(
        _gmm1_kernel,
        grid_spec=pltpu.PrefetchScalarGridSpec(
            num_scalar_prefetch=3, grid=(nff, n_tiles),
            in_specs=[
                pl.BlockSpec(memory_space=pl.ANY),
                pl.BlockSpec((None, None, d, tn), lambda n, i, te, na, tok: (layer, te[i], 0, n)),
                pl.BlockSpec((None, None, d, tn), lambda n, i, te, na, tok: (layer, te[i], 0, nff + n)),
                pl.BlockSpec((None, 1, tn), lambda n, i, te, na, tok: (te[i], 0, n)),
                pl.BlockSpec((None, 1, tn), lambda n, i, te, na, tok: (te[i], 0, nff + n)),
            ],
            out_specs=pl.BlockSpec((MOE_TM, tn), lambda n, i, te, na, tok: (i, n)),
            scratch_shapes=[pltpu.VMEM((d, tn), BF16), pltpu.VMEM((d, tn), BF16),
                            pltpu.VMEM((2, MOE_TM, d), F32), pltpu.SemaphoreType.DMA((2,))]),
        out_shape=jax.ShapeDtypeStruct((n_rows, ff), BF16),
        compiler_params=_params(("arbitrary", "arbitrary")),
        name="expert_up")(tile_expert, n_active, row_tok, h, w_gate_up, w_gate_up, b_gate_up_l, b_gate_up_l)
    tn2 = min(1024, d)
    return pl.pallas_call(
        _gmm2_kernel,
        grid_spec=pltpu.PrefetchScalarGridSpec(
            num_scalar_prefetch=2, grid=(d // tn2, n_tiles),
            in_specs=[
                pl.BlockSpec((MOE_TM, ff), lambda n, i, te, na: (tile(i, na), 0)),
                pl.BlockSpec((None, None, ff, tn2), lambda n, i, te, na: (layer, te[i], 0, n)),
                pl.BlockSpec((None, 1, tn2), lambda n, i, te, na: (te[i], 0, n)),
            ],
            out_specs=pl.BlockSpec((MOE_TM, tn2), lambda n, i, te, na: (i, n)),
            scratch_shapes=[pltpu.VMEM((ff, tn2), BF16)]),
        out_shape=jax.ShapeDtypeStruct((n_rows, d), F32),
        compiler_params=_params(("arbitrary", "arbitrary")),
        name="expert_down")(tile_expert, n_active, act, w_down, b_down_l)


def _combine_ln_kernel(alpha, n_steps, dest_ref, h_ref, gate_ref, g_ref, b_ref, y_hbm, o_ref, ob_ref,
                       ybuf_ref, sem_ref):
    i = pl.program_id(0)
    slot = i % 2
    n_gather = TOP_K * COMBINE_TM

    @pl.when(i == 0)
    def _():
        _gather_rows(y_hbm, dest_ref, 0, n_gather, ybuf_ref.at[0], sem_ref.at[0])

    @pl.when(i + 1 < n_steps)
    def _():
        _gather_rows(y_hbm, dest_ref, (i + 1) * n_gather, n_gather, ybuf_ref.at[1 - slot], sem_ref.at[1 - slot])

    _wait_rows(y_hbm, n_gather, ybuf_ref.at[slot], sem_ref.at[slot])
    moe = None
    for k in range(TOP_K):
        term = gate_ref[:, k:k + 1] * ybuf_ref[slot, pl.ds(k * COMBINE_TM, COMBINE_TM), :]
        moe = term if moe is None else moe + term
    y = _ln_rows(alpha * h_ref[...] + moe, g_ref[...], b_ref[...])
    o_ref[...] = y
    ob_ref[...] = y.astype(BF16)


def _combine_ln(h, y_rows, dest_km, gate, g, b, alpha):
    rows, d = h.shape
    n_steps = rows // COMBINE_TM
    row_spec = pl.BlockSpec((COMBINE_TM, d), lambda i, dest: (i, 0))
    vec_spec = pl.BlockSpec((1, d), lambda i, dest: (0, 0))
    return pl.pallas_call(
        functools.partial(_combine_ln_kernel, alpha, n_steps),
        grid_spec=pltpu.PrefetchScalarGridSpec(
            num_scalar_prefetch=1, grid=(n_steps,),
            in_specs=[row_spec, pl.BlockSpec((COMBINE_TM, HEAD_W), lambda i, dest: (i, 0)),
                      vec_spec, vec_spec, pl.BlockSpec(memory_space=pl.ANY)],
            out_specs=(row_spec, row_spec),
            scratch_shapes=[pltpu.VMEM((2, TOP_K * COMBINE_TM, d), F32), pltpu.SemaphoreType.DMA((2,))]),
        out_shape=(jax.ShapeDtypeStruct((rows, d), F32), jax.ShapeDtypeStruct((rows, d), BF16)),
        compiler_params=_params(("arbitrary",)),
        name="combine_ln")(dest_km, h, gate, g.reshape(1, d), b.reshape(1, d), y_rows)


def _moe_ln(h, w_router_l, b_router_l, w_gate_up, b_gate_up_l, w_down, b_down_l, layer, g, b, alpha):
    rows, d = h.shape
    n_e = w_router_l.shape[-1]
    assert rows % COMBINE_TM == 0
    idx, gate, rank, counts = _router(h, w_router_l, b_router_l)
    idx, rank, counts = idx[:, :TOP_K], rank[:, :TOP_K], counts[0]
    padded = (counts + MOE_TM - 1) // MOE_TM * MOE_TM
    group_end = jnp.cumsum(padded)
    group_start = group_end - padded
    dest = group_start[idx] + rank
    n_tiles = (rows * TOP_K + n_e * (MOE_TM - 1)) // MOE_TM + 1
    n_rows = n_tiles * MOE_TM
    n_active = jnp.maximum(group_end[-1] // MOE_TM, 1).astype(I32)
    tile_ids = jnp.minimum(jnp.arange(n_tiles, dtype=I32), n_active - 1)
    tile_expert = jnp.minimum(jnp.searchsorted(group_end, tile_ids * MOE_TM, side="right"),
                              n_e - 1).astype(I32)
    tok = jnp.broadcast_to(jnp.arange(rows, dtype=I32)[:, None], (rows, TOP_K))
    row_tok = jnp.zeros((n_rows,), I32).at[dest.reshape(-1)].set(tok.reshape(-1))
    y_rows = _experts(h, row_tok, tile_expert, n_active.reshape(1), w_gate_up,
                      b_gate_up_l.reshape(n_e, 1, -1), w_down, b_down_l.reshape(n_e, 1, -1), layer)
    dest_km = jnp.swapaxes(dest.reshape(rows // COMBINE_TM, COMBINE_TM, TOP_K), 1, 2).reshape(-1)
    return _combine_ln(h, y_rows, dest_km.astype(I32), gate, g, b, alpha)


def _rotary_tables(pos):
    half = A_HEAD_DIM // 2
    inv = ROPE_THETA ** (-jnp.arange(half, dtype=F32) / half)
    ang = pos.astype(F32)[:, None] * inv[None, :]
    cos, sin = jnp.cos(ang), jnp.sin(ang)
    reps = HEAD_W // A_HEAD_DIM
    return (jnp.concatenate([cos, cos] * reps, axis=1), jnp.concatenate([-sin, sin] * reps, axis=1))


def _sample_band_bias(rel_bias_l, q_pos, k_pos):
    rel = np.clip(q_pos[:, None] - k_pos[None, :], -REL_CLIP, REL_CLIP) + REL_CLIP
    qc, kc = q_pos[:, None] // CHUNK, k_pos[None, :] // CHUNK
    visible = (kc <= qc) & (kc >= qc - BAND_CHUNKS) & (k_pos[None, :] >= 0)
    return jnp.where(visible[None], rel_bias_l[:, rel], NEG_INF).astype(F32)


def kernel(x_prompt, x_sample, cache_a_k, cache_a_v, cache_b_k, cache_b_v, cache_c_k, cache_c_v, cache_c_logf, ln_in_g, ln_in_b, w_in, b_forget, diff_lambda, diff_subln_g, rel_bias, w_branch_a, w_branch_b, w_branch_c, w_gate, b_gate, w_out, ln_mix_g, ln_mix_b, w_router, b_router, w_gate_up, b_gate_up, w_down, b_down, ln_ffn_g, ln_ffn_b):
    n_p, t_p, d = x_prompt.shape
    n_s, t_s, _ = x_sample.shape
    depth, _, past = cache_a_k.shape[:3]
    band = cache_b_k.shape[2]
    assert n_p == 1 and past % CHUNK == 0 and t_s <= CHUNK and band == BAND_CHUNKS * CHUNK
    rows_s = n_s * t_s
    alpha = (2.0 * depth) ** 0.25

    x = jnp.concatenate([x_prompt.reshape(t_p, d), x_sample.reshape(rows_s, d)], axis=0)
    pos = jnp.concatenate([jnp.arange(t_p), jnp.tile(past + jnp.arange(t_s), n_s)])
    cos_t, sin_t = _rotary_tables(pos)
    flat = lambda c: c.reshape(c.shape[:3] + (WIDTH,))
    cache_a_k, cache_a_v, cache_b_k, cache_b_v, cache_c_k, cache_c_v = map(
        flat, (cache_a_k, cache_a_v, cache_b_k, cache_b_v, cache_c_k, cache_c_v))
    q_pos_s = past + np.arange(t_s)
    k_pos_band = past - band + np.arange(band + t_s)
    pad_t = (-(past + t_s)) % HEAD_W

    h, hb = _layer_norm(x, ln_in_g, ln_in_b)
    states = []
    for l in range(depth):
        lam_init = 0.8 - 0.6 * math.exp(-0.3 * l)
        gain = (diff_subln_g[l] * (1.0 - lam_init)).reshape(1, HEAD_W)
        dl = diff_lambda[l]

        seg_f, seg_b = _project(hb, w_in, l, cos_t, sin_t)
        w_f = jnp.pad(w_in[l, :, N_SEG * WIDTH:], ((0, 0), (0, HEAD_W - N_HEADS)))
        b_f = jnp.pad(b_forget[l], (0, HEAD_W - N_HEADS)).reshape(1, HEAD_W)
        logf = _forget_logf(h, w_f, b_f)[:, :N_HEADS]
        logf_p, logf_s = logf[:t_p], logf[t_p:].reshape(n_s, t_s, N_HEADS)

        cum_p_t = _cumsum_lanes(logf_p.T)
        o_a_p = _prompt_flash("diff", seg_b, SEG_AQ, t_p, (dl, gain), lam_init)
        o_b_p = _prompt_band(seg_b, t_p, _band_bias_window(rel_bias[l]))
        o_c_p = _prompt_flash("forget", seg_b, SEG_CQ, t_p,
                              (cum_p_t.T, cum_p_t.reshape(N_HEADS, 1, t_p)))

        logf_all = jnp.concatenate([cache_c_logf[l], logf_s], axis=1)
        logf_all = jnp.pad(jnp.swapaxes(logf_all, 1, 2), ((0, 0), (0, 0), (0, pad_t)))
        cum_s = _cumsum_lanes(logf_all.reshape(n_s * N_HEADS, -1)).reshape(n_s, N_HEADS, -1)
        cum_new = cum_s[:, :, past:past + t_s]
        bias_s = _sample_band_bias(rel_bias[l], q_pos_s, k_pos_band)
        o_a_s = _sample_attn("diff", seg_b, SEG_AQ, cache_a_k, cache_a_v, l, t_p, (dl, gain), lam_init)
        o_b_s = _sample_attn("band", seg_b, SEG_BQ, cache_b_k, cache_b_v, l, t_p,
                             (bias_s[:, :, :band], bias_s[:, :, band:]))
        o_c_s = _sample_attn("forget", seg_b, SEG_CQ, cache_c_k, cache_c_v, l, t_p,
                             (jnp.swapaxes(cum_new, 1, 2), cum_s[:, :, :past], cum_new))

        o_a = jnp.concatenate([o_a_p, o_a_s], axis=0)
        o_b = jnp.concatenate([o_b_p, o_b_s], axis=0)
        o_c = jnp.concatenate([o_c_p, o_c_s], axis=0)
        mixed = _mix(hb, o_a, o_b, o_c, w_gate, b_gate[l].reshape(N_BRANCHES, d),
                     w_branch_a, w_branch_b, w_branch_c, l)
        h, hb = _layer_norm(_matmul(mixed, w_out, l), ln_mix_g[l], ln_mix_b[l], h, alpha)
        h, hb = _moe_ln(h, w_router[l], b_router[l], w_gate_up, b_gate_up[l], w_down, b_down[l], l,
                        ln_ffn_g[l], ln_ffn_b[l], alpha)
        states.append((seg_f, logf_p, logf_s))

    def prompt_state(seg, keep=None):
        out = jnp.stack([st[0][seg, :t_p] for st in states]).reshape(depth, 1, t_p, N_HEADS, HEAD_W)
        return out if keep is None else out[:, :, -keep:]

    def sample_state(seg):
        return jnp.stack([st[0][seg, t_p:] for st in states]).reshape(depth, n_s, t_s, N_HEADS, HEAD_W)

    band_p = min(BAND_CHUNKS * CHUNK, t_p)
    return (h[:t_p].reshape(1, t_p, d), h[t_p:].reshape(n_s, t_s, d),
            prompt_state(SEG_AK), prompt_state(SEG_AV),
            prompt_state(SEG_BK, band_p), prompt_state(SEG_BV, band_p),
            prompt_state(SEG_CK), prompt_state(SEG_CV),
            jnp.stack([st[1] for st in states]).reshape(depth, 1, t_p, N_HEADS),
            sample_state(SEG_AK), sample_state(SEG_AV), sample_state(SEG_BK), sample_state(SEG_BV),
            sample_state(SEG_CK), sample_state(SEG_CV),
            jnp.stack([st[2] for st in states]))
```

```python
import functools
import math

import numpy as np
import jax
import jax.numpy as jnp
from jax import lax
from jax.experimental import pallas as pl
from jax.experimental.pallas import tpu as pltpu

F32 = jnp.float32
BF16 = jnp.bfloat16
I32 = jnp.int32

CHUNK = 64
N_HEADS = 8
HEAD_W = 128
WIDTH = N_HEADS * HEAD_W
A_HEAD_DIM = 64
BAND_CHUNKS = 8
REL_CLIP = 128
N_BRANCHES = 3
N_SEG = 9
N_EXPERTS = 32
TOP_K = 4
SWIGLU_LIMIT = 7.0
SWIGLU_ALPHA = 1.702
ROPE_THETA = 10000.0
LN_EPS = 1e-5
NEG_INF = -1e30
VMEM_LIMIT_BYTES = 56 * 1024 * 1024

SEG_AQ, SEG_AK, SEG_AV, SEG_BQ, SEG_BK, SEG_BV, SEG_CQ, SEG_CK, SEG_CV = range(N_SEG)


def _params(sem):
    return pltpu.CompilerParams(dimension_semantics=sem, vmem_limit_bytes=VMEM_LIMIT_BYTES)


def _row_tile(rows, target, mult=16):
    best = None
    for t in range(mult, min(rows, target) + 1, mult):
        if rows % t == 0:
            best = t
    assert best is not None, (rows, target)
    return best


def _dot(a, b):
    return jnp.dot(a, b, preferred_element_type=F32)


def _dot_nt(a, b):
    return lax.dot_general(a, b, (((1,), (1,)), ((), ())), preferred_element_type=F32)


def _dot3(x, w):
    xh = x.astype(BF16)
    xl = (x - xh.astype(F32)).astype(BF16)
    wh = w.astype(BF16)
    wl = (w - wh.astype(F32)).astype(BF16)
    return _dot(xh, wh) + (_dot(xh, wl) + _dot(xl, wh))


def _ln_rows(x, g, b):
    mu = jnp.mean(x, axis=-1, keepdims=True)
    xc = x - mu
    var = jnp.mean(xc * xc, axis=-1, keepdims=True)
    return xc * lax.rsqrt(var + LN_EPS) * g + b


def _ln_kernel(x_ref, g_ref, b_ref, y_ref, yb_ref):
    y = _ln_rows(x_ref[...], g_ref[...], b_ref[...])
    y_ref[...] = y
    yb_ref[...] = y.astype(BF16)


def _add_ln_kernel(alpha, h_ref, z_ref, g_ref, b_ref, y_ref, yb_ref):
    y = _ln_rows(alpha * h_ref[...] + z_ref[...], g_ref[...], b_ref[...])
    y_ref[...] = y
    yb_ref[...] = y.astype(BF16)


def _layer_norm(x, g, b, h=None, alpha=None):
    rows, d = x.shape
    tm = _row_tile(rows, 512)
    row_spec = pl.BlockSpec((tm, d), lambda i: (i, 0))
    vec_spec = pl.BlockSpec((1, d), lambda i: (0, 0))
    g2, b2 = g.reshape(1, d), b.reshape(1, d)
    out_shape = (jax.ShapeDtypeStruct((rows, d), F32), jax.ShapeDtypeStruct((rows, d), BF16))
    if h is None:
        return pl.pallas_call(
            _ln_kernel, grid=(rows // tm,), in_specs=[row_spec, vec_spec, vec_spec],
            out_specs=(row_spec, row_spec), out_shape=out_shape,
            compiler_params=_params(("parallel",)), name="ln_in")(x, g2, b2)
    return pl.pallas_call(
        functools.partial(_add_ln_kernel, alpha), grid=(rows // tm,),
        in_specs=[row_spec, row_spec, vec_spec, vec_spec],
        out_specs=(row_spec, row_spec), out_shape=out_shape,
        compiler_params=_params(("parallel",)), name="add_ln")(h, x, g2, b2)


def _proj_kernel(qscale, hb_ref, w_ref, cos_ref, sin_ref, f_ref, b_ref, wb_ref):
    j = pl.program_id(0)

    @pl.when(pl.program_id(1) == 0)
    def _():
        wb_ref[...] = w_ref[...].astype(BF16)

    acc = _dot(hb_ref[...], wb_ref[...])

    @pl.when(j <= SEG_AK)
    def _():
        cos = jnp.concatenate([cos_ref[...]] * N_HEADS, axis=1)
        sin = jnp.concatenate([sin_ref[...]] * N_HEADS, axis=1)
        lane = lax.broadcasted_iota(I32, acc.shape, 1)
        first_half = (lane % A_HEAD_DIM) < (A_HEAD_DIM // 2)
        partner = jnp.where(first_half,
                            pltpu.roll(acc, WIDTH - A_HEAD_DIM // 2, 1),
                            pltpu.roll(acc, A_HEAD_DIM // 2, 1))
        rot = acc * cos + partner * sin
        f_ref[...] = rot
        scale = jnp.where(j == SEG_AQ, qscale[SEG_AQ], 1.0)
        b_ref[...] = (rot * scale).astype(BF16)

    @pl.when(j > SEG_AK)
    def _():
        f_ref[...] = acc
        scale = jnp.where((j == SEG_BQ) | (j == SEG_CQ), qscale[SEG_BQ], 1.0)
        b_ref[...] = (acc * scale).astype(BF16)


def _project(hb, w_in, layer, cos_t, sin_t):
    rows, d = hb.shape
    tm = _row_tile(rows, 640)
    qscale = {SEG_AQ: A_HEAD_DIM ** -0.5, SEG_BQ: HEAD_W ** -0.5}
    return pl.pallas_call(
        functools.partial(_proj_kernel, qscale),
        grid=(N_SEG, rows // tm),
        in_specs=[
            pl.BlockSpec((tm, d), lambda j, i: (i, 0)),
            pl.BlockSpec((None, d, WIDTH), lambda j, i: (layer, 0, j)),
            pl.BlockSpec((tm, HEAD_W), lambda j, i: (i, 0)),
            pl.BlockSpec((tm, HEAD_W), lambda j, i: (i, 0)),
        ],
        out_specs=(pl.BlockSpec((None, tm, WIDTH), lambda j, i: (j, i, 0)),
                   pl.BlockSpec((None, tm, WIDTH), lambda j, i: (j, i, 0))),
        out_shape=(jax.ShapeDtypeStruct((N_SEG, rows, WIDTH), F32),
                   jax.ShapeDtypeStruct((N_SEG, rows, WIDTH), BF16)),
        scratch_shapes=[pltpu.VMEM((d, WIDTH), BF16)],
        compiler_params=_params(("arbitrary", "arbitrary")), name="in_proj")(hb, w_in, cos_t, sin_t)


def _forget_kernel(h_ref, w_ref, bias_ref, o_ref):
    x = _dot3(h_ref[...], w_ref[...]) + bias_ref[...]
    o_ref[...] = jnp.minimum(x, 0.0) - jnp.log(1.0 + jnp.exp(-jnp.abs(x)))


def _forget_logf(h, w_f, b_f):
    rows, d = h.shape
    tm = _row_tile(rows, 512)
    return pl.pallas_call(
        _forget_kernel, grid=(rows // tm,),
        in_specs=[pl.BlockSpec((tm, d), lambda i: (i, 0)),
                  pl.BlockSpec((d, HEAD_W), lambda i: (0, 0)),
                  pl.BlockSpec((1, HEAD_W), lambda i: (0, 0))],
        out_specs=pl.BlockSpec((tm, HEAD_W), lambda i: (i, 0)),
        out_shape=jax.ShapeDtypeStruct((rows, HEAD_W), F32),
        compiler_params=_params(("parallel",)), name="forget_gate")(h, w_f, b_f)


def _cumsum_kernel(x_ref, o_ref):
    x = x_ref[...]
    n = x.shape[1]
    lane = lax.broadcasted_iota(I32, x.shape, 1)
    shift = 1
    while shift < n:
        x = x + jnp.where(lane >= shift, pltpu.roll(x, shift, 1), 0.0)
        shift *= 2
    o_ref[...] = x


def _cumsum_lanes(x):
    return pl.pallas_call(
        _cumsum_kernel, out_shape=jax.ShapeDtypeStruct(x.shape, F32),
        compiler_params=pltpu.CompilerParams(vmem_limit_bytes=VMEM_LIMIT_BYTES), name="cumsum")(x)


def _lanes(x, n):
    if n <= HEAD_W:
        return x[:, :n]
    return jnp.tile(x, (1, n // HEAD_W))


def _with_ones(v):
    return jnp.concatenate([v, jnp.ones_like(v)], axis=1)


def _softmax_block(s, v1, m_ref, acc_ref):
    m_old = m_ref[...]
    m_new = jnp.maximum(m_old, jnp.max(s, axis=-1, keepdims=True))
    p = jnp.exp(s - _lanes(m_new, s.shape[1]))
    a = jnp.exp(m_old - m_new)
    acc_ref[...] = _lanes(a, 2 * HEAD_W) * acc_ref[...] + _dot(p.astype(BF16), v1)
    m_ref[...] = m_new


def _init_stats(m_ref, acc_ref):
    m_ref[...] = jnp.full(m_ref.shape, NEG_INF, F32)
    acc_ref[...] = jnp.zeros(acc_ref.shape, F32)


def _normalized(acc):
    return acc[:, :HEAD_W] / acc[:, HEAD_W:]


def _split_halves(q):
    lane = lax.broadcasted_iota(I32, q.shape, 1)
    zero = jnp.zeros_like(q)
    return jnp.concatenate([jnp.where(lane < A_HEAD_DIM, q, zero),
                            jnp.where(lane >= A_HEAD_DIM, q, zero)], axis=0)


def _diff_lambda(dl, lam_init):
    s1 = jnp.sum(dl[0:1, :] * dl[1:2, :], axis=-1, keepdims=True)
    s2 = jnp.sum(dl[2:3, :] * dl[3:4, :], axis=-1, keepdims=True)
    return jnp.exp(s1) - jnp.exp(s2) + lam_init


def _diff_finish(acc, n, lam, gain):
    o = _normalized(acc[:n]) - lam * _normalized(acc[n:])
    return o * lax.rsqrt(jnp.mean(o * o, axis=-1, keepdims=True) + LN_EPS) * gain


FLASH_SMALL = 512
FLASH_BIG = 2048


def _prompt_flash_kernel(kind, tq, big, lam_init, *refs):
    if kind == "diff":
        q_ref, k_ref, v_ref, dl_ref, gain_ref, o_ref, qs_ref, v1_ref, m_ref, acc_ref = refs
    else:
        q_ref, k_ref, v_ref, cq_ref, ck_ref, o_ref, qs_ref, v1_ref, cqr_ref, m_ref, acc_ref = refs
    h = pl.program_id(0)
    i = pl.program_id(1)
    q0 = i * tq
    nq = qs_ref.shape[0]

    @pl.when(i == 0)
    def _():
        v1_ref[...] = _with_ones(v_ref[...])

    if kind == "diff":
        qs_ref[...] = _split_halves(q_ref[...])
    else:
        qs_ref[...] = q_ref[...]
        head_lane = lax.broadcasted_iota(I32, cq_ref.shape, 1) == h
        cq = jnp.sum(jnp.where(head_lane, cq_ref[...], 0.0), axis=-1, keepdims=True)
        cqr_ref[...] = jnp.broadcast_to(cq, cqr_ref.shape)
    _init_stats(m_ref, acc_ref)

    def block(k0, tk, diagonal):
        s = _dot_nt(qs_ref[...], k_ref[pl.ds(k0, tk), :])
        if kind == "forget":
            s = s + _lanes(cqr_ref[...], tk) - ck_ref[:, pl.ds(k0, tk)]
        if diagonal:
            qpos = q0 + lax.broadcasted_iota(I32, (nq, tk), 0) % tq
            kpos = k0 + lax.broadcasted_iota(I32, (nq, tk), 1)
            visible = (kpos // CHUNK) <= (qpos // CHUNK) if kind == "diff" else kpos <= qpos
            s = jnp.where(visible, s, NEG_INF)
        _softmax_block(s, v1_ref[pl.ds(k0, tk), :], m_ref, acc_ref)

    def big_body(kb, carry):
        block(pl.multiple_of(kb * big, big), big, False)
        return carry

    def small_body(kb, carry):
        block(pl.multiple_of(kb * FLASH_SMALL, FLASH_SMALL), FLASH_SMALL, False)
        return carry

    n_big = q0 // big
    n_small = q0 // FLASH_SMALL
    lax.fori_loop(0, n_big, big_body, 0)
    lax.fori_loop(n_big * (big // FLASH_SMALL), n_small, small_body, 0)
    block(pl.multiple_of(n_small * FLASH_SMALL, FLASH_SMALL), FLASH_SMALL, True)

    if kind == "diff":
        lam = _diff_lambda(dl_ref[...], lam_init)
        o = _diff_finish(acc_ref[...], tq, lam, gain_ref[...])
    else:
        o = _normalized(acc_ref[...])
    o_ref[...] = o.astype(BF16)


def _prompt_flash(kind, qkv, seg_q, t, extra, lam_init=0.0):
    tq = 256 if kind == "diff" else 512
    big = min(FLASH_BIG, t)
    assert t % big == 0 and big % FLASH_SMALL == 0 and FLASH_SMALL % tq == 0
    nq = 2 * tq if kind == "diff" else tq
    in_specs = [
        pl.BlockSpec((None, tq, HEAD_W), lambda h, i: (seg_q, i, h)),
        pl.BlockSpec((None, t, HEAD_W), lambda h, i: (seg_q + 1, 0, h)),
        pl.BlockSpec((None, t, HEAD_W), lambda h, i: (seg_q + 2, 0, h)),
    ]
    scratch = [pltpu.VMEM((nq, HEAD_W), BF16), pltpu.VMEM((t, 2 * HEAD_W), BF16)]
    if kind == "diff":
        dl, gain = extra
        in_specs += [pl.BlockSpec(dl.shape, lambda h, i: (0, 0)),
                     pl.BlockSpec(gain.shape, lambda h, i: (0, 0))]
    else:
        cum, cum_t = extra
        in_specs += [pl.BlockSpec((tq, N_HEADS), lambda h, i: (i, 0)),
                     pl.BlockSpec((None, 1, t), lambda h, i: (h, 0, 0))]
        scratch += [pltpu.VMEM((tq, HEAD_W), F32)]
    scratch += [pltpu.VMEM((nq, HEAD_W), F32), pltpu.VMEM((nq, 2 * HEAD_W), F32)]
    return pl.pallas_call(
        functools.partial(_prompt_flash_kernel, kind, tq, big, lam_init),
        grid=(N_HEADS, t // tq), in_specs=in_specs,
        out_specs=pl.BlockSpec((tq, HEAD_W), lambda h, i: (i, h)),
        out_shape=jax.ShapeDtypeStruct((t, WIDTH), BF16),
        scratch_shapes=scratch,
        compiler_params=_params(("arbitrary", "arbitrary")),
        name="prompt_" + kind)(qkv, qkv, qkv, *extra)


BAND_TQ = 256
BAND_PAD = BAND_CHUNKS * CHUNK
BAND_WIN = BAND_PAD + BAND_TQ


def _prompt_band_kernel(q_ref, k_ref, v_ref, bias_ref, o_ref, kp_ref, v1_ref):
    i = pl.program_id(1)

    @pl.when(i == 0)
    def _():
        kp_ref[:BAND_PAD, :] = jnp.zeros((BAND_PAD, HEAD_W), BF16)
        kp_ref[BAND_PAD:, :] = k_ref[...]
        v1_ref[:BAND_PAD, :] = jnp.zeros((BAND_PAD, 2 * HEAD_W), BF16)
        v1_ref[BAND_PAD:, :] = _with_ones(v_ref[...])

    r0 = pl.multiple_of(i * BAND_TQ, BAND_TQ)
    s = _dot_nt(q_ref[...], kp_ref[pl.ds(r0, BAND_WIN), :]) + bias_ref[...]
    kpos = r0 - BAND_PAD + lax.broadcasted_iota(I32, s.shape, 1)
    s = jnp.where(kpos >= 0, s, NEG_INF)
    m = jnp.broadcast_to(jnp.max(s, axis=-1, keepdims=True), (BAND_TQ, HEAD_W))
    p = jnp.exp(s - _lanes(m, BAND_WIN))
    o_ref[...] = _normalized(_dot(p.astype(BF16), v1_ref[pl.ds(r0, BAND_WIN), :])).astype(BF16)


def _band_bias_window(rel_bias_l):
    n_tab = rel_bias_l.shape[0]
    span = BAND_TQ + BAND_WIN - 1
    dist = BAND_PAD + (BAND_TQ - 1) - np.arange(span)
    e = rel_bias_l[:, np.clip(dist, -REL_CLIP, REL_CLIP) + REL_CLIP]
    period = span + 1
    f = jnp.concatenate([e[:, BAND_TQ - 1:], jnp.zeros((n_tab, 1), e.dtype), e[:, :BAND_TQ - 1]], axis=1)
    skew = jnp.tile(f, (1, BAND_TQ))[:, :BAND_TQ * span].reshape(n_tab, BAND_TQ, span)
    bias = skew[:, :, :BAND_WIN]
    r = np.arange(BAND_TQ)[:, None]
    c = np.arange(BAND_WIN)[None, :] - BAND_PAD
    visible = (c // CHUNK <= r // CHUNK) & (c // CHUNK >= r // CHUNK - BAND_CHUNKS)
    assert period == span + 1
    return jnp.where(visible[None], bias, NEG_INF).astype(F32)


def _prompt_band(qkv, t, bias):
    assert t % BAND_TQ == 0 and BAND_TQ % CHUNK == 0
    return pl.pallas_call(
        _prompt_band_kernel, grid=(N_HEADS, t // BAND_TQ),
        in_specs=[
            pl.BlockSpec((None, BAND_TQ, HEAD_W), lambda h, i: (SEG_BQ, i, h)),
            pl.BlockSpec((None, t, HEAD_W), lambda h, i: (SEG_BK, 0, h)),
            pl.BlockSpec((None, t, HEAD_W), lambda h, i: (SEG_BV, 0, h)),
            pl.BlockSpec((None, BAND_TQ, BAND_WIN), lambda h, i: (h, 0, 0)),
        ],
        out_specs=pl.BlockSpec((BAND_TQ, HEAD_W), lambda h, i: (i, h)),
        out_shape=jax.ShapeDtypeStruct((t, WIDTH), BF16),
        scratch_shapes=[pltpu.VMEM((t + BAND_PAD, HEAD_W), BF16), pltpu.VMEM((t + BAND_PAD, 2 * HEAD_W), BF16)],
        compiler_params=_params(("arbitrary", "arbitrary")), name="prompt_band")(qkv, qkv, qkv, bias)


def _sample_attn_kernel(kind, layer, n_kt, tk, ts, lam_init, *refs):
    *refs, kbuf_ref, vbuf_ref, sem_ref = refs
    if kind == "diff":
        (q_ref, kc_hbm, vc_hbm, kn_ref, vn_ref, dl_ref, gain_ref, o_ref, qs_ref, m_ref, acc_ref) = refs
    elif kind == "band":
        (q_ref, kc_hbm, vc_hbm, kn_ref, vn_ref, bc_ref, bn_ref, o_ref, qs_ref, m_ref, acc_ref) = refs
    else:
        (q_ref, kc_hbm, vc_hbm, kn_ref, vn_ref, cq_ref, ckc_ref, ckn_ref, o_ref, qs_ref, m_ref, acc_ref) = refs
    kt = pl.program_id(1)
    nq = qs_ref.shape[1]
    step = pl.program_id(0) * n_kt + kt
    n_steps = pl.num_programs(0) * n_kt
    slot = step % 2

    def head_copies(stream, tile, buf):
        rows = pl.ds(pl.multiple_of(tile * tk, tk), tk)
        copies = []
        for h in range(N_HEADS):
            copies.append(pltpu.make_async_copy(kc_hbm.at[layer, stream, rows, h, :], kbuf_ref.at[buf, h],
                                                sem_ref.at[0, buf]))
            copies.append(pltpu.make_async_copy(vc_hbm.at[layer, stream, rows, h, :], vbuf_ref.at[buf, h],
                                                sem_ref.at[1, buf]))
        return copies

    @pl.when(step == 0)
    def _():
        for c in head_copies(0, 0, 0):
            c.start()

    @pl.when(step + 1 < n_steps)
    def _():
        for c in head_copies((step + 1) // n_kt, (step + 1) % n_kt, 1 - slot):
            c.start()

    for c in head_copies(pl.program_id(0), kt, slot):
        c.wait()

    @pl.when(kt == 0)
    def _():
        for h in range(N_HEADS):
            q = q_ref[:, h * HEAD_W:(h + 1) * HEAD_W]
            qs_ref[h] = _split_halves(q) if kind == "diff" else q
        _init_stats(m_ref, acc_ref)

    def head_bias(h, s, new):
        if kind == "band":
            return s + (bn_ref[h] if new else bc_ref[h])
        if kind == "forget":
            cq = cq_ref[:, h:h + 1]
            ck = ckn_ref[h:h + 1, :] if new else ckc_ref[h:h + 1, :]
            return s + cq - ck
        return s

    for h in range(N_HEADS):
        s = _dot_nt(qs_ref[h], kbuf_ref[slot, h].astype(BF16))
        _softmax_block(head_bias(h, s, False), _with_ones(vbuf_ref[slot, h].astype(BF16)),
                       m_ref.at[h], acc_ref.at[h])

    @pl.when(kt == n_kt - 1)
    def _():
        for h in range(N_HEADS):
            sl = slice(h * HEAD_W, (h + 1) * HEAD_W)
            s = head_bias(h, _dot_nt(qs_ref[h], kn_ref[:, sl]), True)
            if kind == "forget":
                r = lax.broadcasted_iota(I32, (nq, ts), 0)
                c = lax.broadcasted_iota(I32, (nq, ts), 1)
                s = jnp.where(c <= r, s, NEG_INF)
            _softmax_block(s, _with_ones(vn_ref[:, sl]), m_ref.at[h], acc_ref.at[h])
            if kind == "diff":
                lam = _diff_lambda(dl_ref[...], lam_init)
                o = _diff_finish(acc_ref[h], ts, lam, gain_ref[...])
            else:
                o = _normalized(acc_ref[h])
            o_ref[:, sl] = o.astype(BF16)


def _sample_attn(kind, qkv, seg_q, cache_k, cache_v, layer, t_prompt, extra, lam_init=0.0):
    _, n_streams, past = cache_k.shape[:3]
    rows = qkv.shape[1]
    ts = (rows - t_prompt) // n_streams
    assert t_prompt % ts == 0 and ts % 16 == 0
    tk = _row_tile(past, 1024, 128)
    n_kt = past // tk
    nq = 2 * ts if kind == "diff" else ts
    row0 = t_prompt // ts
    new_spec = lambda seg: pl.BlockSpec((None, ts, WIDTH), lambda b, kt: (seg, row0 + b, 0))
    cache_spec = pl.BlockSpec(memory_space=pl.ANY)
    in_specs = [new_spec(seg_q), cache_spec, cache_spec, new_spec(seg_q + 1), new_spec(seg_q + 2)]
    if kind == "diff":
        dl, gain = extra
        in_specs += [pl.BlockSpec(dl.shape, lambda b, kt: (0, 0)),
                     pl.BlockSpec(gain.shape, lambda b, kt: (0, 0))]
    elif kind == "band":
        bias_c, bias_n = extra
        in_specs += [pl.BlockSpec((N_HEADS, ts, tk), lambda b, kt: (0, 0, kt)),
                     pl.BlockSpec(bias_n.shape, lambda b, kt: (0, 0, 0))]
    else:
        cq, ck_cache, ck_new = extra
        in_specs += [pl.BlockSpec((None, ts, N_HEADS), lambda b, kt: (b, 0, 0)),
                     pl.BlockSpec((None, N_HEADS, tk), lambda b, kt: (b, 0, kt)),
                     pl.BlockSpec((None, N_HEADS, ts), lambda b, kt: (b, 0, 0))]
    return pl.pallas_call(
        functools.partial(_sample_attn_kernel, kind, layer, n_kt, tk, ts, lam_init),
        grid=(n_streams, n_kt), in_specs=in_specs,
        out_specs=pl.BlockSpec((ts, WIDTH), lambda b, kt: (b, 0)),
        out_shape=jax.ShapeDtypeStruct((n_streams * ts, WIDTH), BF16),
        scratch_shapes=[pltpu.VMEM((N_HEADS, nq, HEAD_W), BF16), pltpu.VMEM((N_HEADS, nq, HEAD_W), F32),
                        pltpu.VMEM((N_HEADS, nq, 2 * HEAD_W), F32),
                        pltpu.VMEM((2, N_HEADS, tk, HEAD_W), F32), pltpu.VMEM((2, N_HEADS, tk, HEAD_W), F32),
                        pltpu.SemaphoreType.DMA((2, 2))],
        compiler_params=_params(("arbitrary", "arbitrary")),
        name="sample_" + kind)(qkv, cache_k, cache_v, qkv, qkv, *extra)


def _mix_kernel(hb_ref, oa_ref, ob_ref, oc_ref, wg0_ref, wg1_ref, wg2_ref, bg_ref,
                wa_ref, wb_ref, wc_ref, o_ref, wgs_ref, wbs_ref):
    @pl.when(pl.program_id(1) == 0)
    def _():
        for b, (wg, wbr) in enumerate(((wg0_ref, wa_ref), (wg1_ref, wb_ref), (wg2_ref, wc_ref))):
            wgs_ref[b] = wg[...].astype(BF16)
            wbs_ref[b] = wbr[...].astype(BF16)

    hb = hb_ref[...]
    mixed = None
    for b, o in enumerate((oa_ref, ob_ref, oc_ref)):
        gate = jax.nn.sigmoid(_dot(hb, wgs_ref[b]) + bg_ref[b:b + 1, :])
        term = gate * _dot(o[...], wbs_ref[b])
        mixed = term if mixed is None else mixed + term
    o_ref[...] = mixed.astype(BF16)


def _mix(hb, o_a, o_b, o_c, w_gate, b_gate_l, w_a, w_b, w_c, layer):
    rows, d = hb.shape
    tm = _row_tile(rows, 640)
    tn = 256
    nb = d // tn
    gate_spec = lambda b: pl.BlockSpec((None, d, tn), lambda n, i: (layer, 0, b * nb + n))
    branch_spec = pl.BlockSpec((None, WIDTH, tn), lambda n, i: (layer, 0, n))
    o_spec = pl.BlockSpec((tm, WIDTH), lambda n, i: (i, 0))
    return pl.pallas_call(
        _mix_kernel, grid=(nb, rows // tm),
        in_specs=[pl.BlockSpec((tm, d), lambda n, i: (i, 0)), o_spec, o_spec, o_spec,
                  gate_spec(0), gate_spec(1), gate_spec(2),
                  pl.BlockSpec((N_BRANCHES, tn), lambda n, i: (0, n)),
                  branch_spec, branch_spec, branch_spec],
        out_specs=pl.BlockSpec((tm, tn), lambda n, i: (i, n)),
        out_shape=jax.ShapeDtypeStruct((rows, d), BF16),
        scratch_shapes=[pltpu.VMEM((N_BRANCHES, d, tn), BF16), pltpu.VMEM((N_BRANCHES, WIDTH, tn), BF16)],
        compiler_params=_params(("arbitrary", "arbitrary")),
        name="branch_mix")(hb, o_a, o_b, o_c, w_gate, w_gate, w_gate, b_gate_l, w_a, w_b, w_c)


def _matmul_kernel(x_ref, w_ref, o_ref, wb_ref):
    @pl.when(pl.program_id(1) == 0)
    def _():
        wb_ref[...] = w_ref[...].astype(BF16)
    o_ref[...] = _dot(x_ref[...], wb_ref[...])


def _matmul(x, w, layer):
    rows, k = x.shape
    n = w.shape[-1]
    tm = _row_tile(rows, 640)
    tn = min(512, n)
    assert n % tn == 0
    return pl.pallas_call(
        _matmul_kernel, grid=(n // tn, rows // tm),
        in_specs=[pl.BlockSpec((tm, k), lambda j, i: (i, 0)),
                  pl.BlockSpec((None, k, tn), lambda j, i: (layer, 0, j))],
        out_specs=pl.BlockSpec((tm, tn), lambda j, i: (i, j)),
        out_shape=jax.ShapeDtypeStruct((rows, n), F32),
        scratch_shapes=[pltpu.VMEM((k, tn), BF16)],
        compiler_params=_params(("arbitrary", "arbitrary")), name="out_proj")(x, w)


def _router_kernel(h_ref, w_ref, b_ref, idx_ref, gate_ref, rank_ref, cnt_ref, run_ref):
    @pl.when(pl.program_id(0) == 0)
    def _():
        run_ref[...] = jnp.zeros(run_ref.shape, F32)

    logits = _dot3(h_ref[...], w_ref[...]) + b_ref[...]
    tm, n_e = logits.shape
    e_iota = lax.broadcasted_iota(I32, (tm, n_e), 1)
    out_lane = lax.broadcasted_iota(I32, (tm, HEAD_W), 1)
    vals, hots, idxs = [], [], []
    for _ in range(TOP_K):
        top = jnp.max(logits, axis=-1, keepdims=True)
        idx = jnp.min(jnp.where(logits == top, e_iota, n_e), axis=-1, keepdims=True)
        hot = e_iota == idx
        vals.append(top)
        idxs.append(idx)
        hots.append(hot)
        logits = jnp.where(hot, -jnp.inf, logits)
    exps = [jnp.exp(v - vals[0]) for v in vals]
    denom = exps[0] + exps[1] + exps[2] + exps[3]

    chosen = hots[0] | hots[1] | hots[2] | hots[3]
    chosen_f = jnp.where(chosen, 1.0, 0.0)
    r = lax.broadcasted_iota(I32, (tm, tm), 0)
    c = lax.broadcasted_iota(I32, (tm, tm), 1)
    before = jnp.where(c < r, 1.0, 0.0).astype(BF16)
    rank = _dot(before, chosen_f.astype(BF16)) + run_ref[...]
    run_ref[...] = run_ref[...] + jnp.sum(chosen_f, axis=0, keepdims=True)

    idx_out = jnp.zeros((tm, HEAD_W), I32)
    gate_out = jnp.zeros((tm, HEAD_W), F32)
    rank_out = jnp.zeros((tm, HEAD_W), I32)
    for k in range(TOP_K):
        rk = jnp.sum(jnp.where(hots[k], rank, 0.0), axis=-1, keepdims=True).astype(I32)
        idx_out = jnp.where(out_lane == k, idxs[k], idx_out)
        gate_out = jnp.where(out_lane == k, exps[k] / denom, gate_out)
        rank_out = jnp.where(out_lane == k, rk, rank_out)
    idx_ref[...] = idx_out
    gate_ref[...] = gate_out
    rank_ref[...] = rank_out
    cnt_ref[...] = run_ref[...].astype(I32)


def _router(h, w_router_l, b_router_l):
    rows, d = h.shape
    n_e = w_router_l.shape[-1]
    tm = _row_tile(rows, 256)
    lanes = pl.BlockSpec((tm, HEAD_W), lambda i: (i, 0))
    return pl.pallas_call(
        _router_kernel, grid=(rows // tm,),
        in_specs=[pl.BlockSpec((tm, d), lambda i: (i, 0)),
                  pl.BlockSpec((d, n_e), lambda i: (0, 0)),
                  pl.BlockSpec((1, n_e), lambda i: (0, 0))],
        out_specs=(lanes, lanes, lanes, pl.BlockSpec((1, n_e), lambda i: (0, 0))),
        out_shape=(jax.ShapeDtypeStruct((rows, HEAD_W), I32), jax.ShapeDtypeStruct((rows, HEAD_W), F32),
                   jax.ShapeDtypeStruct((rows, HEAD_W), I32), jax.ShapeDtypeStruct((1, n_e), I32)),
        scratch_shapes=[pltpu.VMEM((1, n_e), F32)],
        compiler_params=_params(("arbitrary",)), name="router")(h, w_router_l, b_router_l.reshape(1, n_e))


MOE_TM = 256
COMBINE_TM = 128


def _expert_changed(te_ref, i):
    return (i == 0) | (te_ref[i] != te_ref[jnp.maximum(i - 1, 0)])


def _gather_rows(src_hbm, idx_ref, base, n_rows, dst, sem):
    for r in range(n_rows):
        pltpu.make_async_copy(src_hbm.at[pl.ds(idx_ref[base + r], 1)], dst.at[pl.ds(r, 1)], sem).start()


def _wait_rows(src_hbm, n_rows, dst, sem):
    pltpu.make_async_copy(src_hbm.at[pl.ds(0, n_rows)], dst, sem).wait()


def _weight_copy(w_hbm, layer, expert, col0, tn, stage_ref, sem):
    return pltpu.make_async_copy(w_hbm.at[layer, expert, :, pl.ds(col0, tn)], stage_ref, sem)


def _staged_weights(copies_for, te_ref, nx_ref, n, i, live, cast):
    @pl.when((n == 0) & (i == 0))
    def _():
        for c in copies_for(te_ref[0], 0):
            c.start()

    @pl.when(live & _expert_changed(te_ref, i))
    def _():
        for c in copies_for(te_ref[i], n):
            c.wait()
        cast()
        nxt = nx_ref[i]
        more = nxt < N_EXPERTS

        @pl.when(more)
        def _():
            for c in copies_for(nxt, n):
                c.start()

        @pl.when(jnp.logical_not(more) & (n + 1 < pl.num_programs(0)))
        def _():
            for c in copies_for(te_ref[0], n + 1):
                c.start()


def _gmm1_kernel(layer, tn, ff, te_ref, nx_ref, na_ref, tok_ref, h_hbm, w_hbm, bg_ref, bu_ref, o_ref,
                 wgs_ref, wus_ref, stg_ref, stu_ref, xbuf_ref, xsem_ref, wsem_ref):
    n = pl.program_id(0)
    i = pl.program_id(1)
    n_active = na_ref[0]
    live = i < n_active
    slot = i % 2

    def copies_for(expert, sweep):
        c0 = pl.multiple_of(sweep * tn, tn)
        return (_weight_copy(w_hbm, layer, expert, c0, tn, stg_ref, wsem_ref.at[0]),
                _weight_copy(w_hbm, layer, expert, ff + c0, tn, stu_ref, wsem_ref.at[1]))

    def cast():
        wgs_ref[...] = stg_ref[...].astype(BF16)
        wus_ref[...] = stu_ref[...].astype(BF16)

    @pl.when(i == 0)
    def _():
        _gather_rows(h_hbm, tok_ref, 0, MOE_TM, xbuf_ref.at[0], xsem_ref.at[0])

    _staged_weights(copies_for, te_ref, nx_ref, n, i, live, cast)

    @pl.when(live)
    def _():
        _wait_rows(h_hbm, MOE_TM, xbuf_ref.at[slot], xsem_ref.at[slot])
        ahead = jnp.minimum(i + 1, n_active - 1)
        _gather_rows(h_hbm, tok_ref, ahead * MOE_TM, MOE_TM, xbuf_ref.at[1 - slot], xsem_ref.at[1 - slot])
        x = xbuf_ref[slot].astype(BF16)
        g = jnp.minimum(_dot(x, wgs_ref[...]) + bg_ref[...], SWIGLU_LIMIT)
        u = jnp.clip(_dot(x, wus_ref[...]) + bu_ref[...], -SWIGLU_LIMIT, SWIGLU_LIMIT)
        o_ref[...] = (g * jax.nn.sigmoid(SWIGLU_ALPHA * g) * (u + 1.0)).astype(BF16)

    @pl.when(i == n_active - 1)
    def _():
        _wait_rows(h_hbm, MOE_TM, xbuf_ref.at[1 - slot], xsem_ref.at[1 - slot])

    @pl.when(jnp.logical_not(live))
    def _():
        o_ref[...] = jnp.zeros(o_ref.shape, o_ref.dtype)


def _gmm2_kernel(layer, tn, te_ref, nx_ref, na_ref, x_ref, w_hbm, b_ref, o_ref, ws_ref, st_ref, wsem_ref):
    n = pl.program_id(0)
    i = pl.program_id(1)
    live = i < na_ref[0]

    def copies_for(expert, sweep):
        return (_weight_copy(w_hbm, layer, expert, pl.multiple_of(sweep * tn, tn), tn, st_ref, wsem_ref.at[0]),)

    def cast():
        ws_ref[...] = st_ref[...].astype(BF16)

    _staged_weights(copies_for, te_ref, nx_ref, n, i, live, cast)

    @pl.when(live)
    def _():
        o_ref[...] = _dot(x_ref[...], ws_ref[...]) + b_ref[...]

    @pl.when(jnp.logical_not(live))
    def _():
        o_ref[...] = jnp.zeros(o_ref.shape, o_ref.dtype)


def _experts(h, row_tok, tile_expert, next_expert, n_active, w_gate_up, b_gate_up_l, w_down, b_down_l, layer):
    n_rows = row_tok.shape[0]
    d = h.shape[1]
    ff = w_down.shape[2]
    n_tiles = n_rows // MOE_TM
    tn = min(1024, ff)
    nff = ff // tn
    tile = lambda i, na: jnp.minimum(i, na[0] - 1)
    act = pl.pallas_call(
        functools.partial(_gmm1_kernel, layer, tn, ff),
        grid_spec=pltpu.PrefetchScalarGridSpec(
            num_scalar_prefetch=4, grid=(nff, n_tiles),
            in_specs=[
                pl.BlockSpec(memory_space=pl.ANY),
                pl.BlockSpec(memory_space=pl.ANY),
                pl.BlockSpec((None, 1, tn), lambda n, i, te, nx, na, tok: (te[i], 0, n)),
                pl.BlockSpec((None, 1, tn), lambda n, i, te, nx, na, tok: (te[i], 0, nff + n)),
            ],
            out_specs=pl.BlockSpec((MOE_TM, tn), lambda n, i, te, nx, na, tok: (i, n)),
            scratch_shapes=[pltpu.VMEM((d, tn), BF16), pltpu.VMEM((d, tn), BF16),
                            pltpu.VMEM((d, tn), F32), pltpu.VMEM((d, tn), F32),
                            pltpu.VMEM((2, MOE_TM, d), F32),
                            pltpu.SemaphoreType.DMA((2,)), pltpu.SemaphoreType.DMA((2,))]),
        out_shape=jax.ShapeDtypeStruct((n_rows, ff), BF16),
        compiler_params=_params(("arbitrary", "arbitrary")),
        name="expert_up")(tile_expert, next_expert, n_active, row_tok, h, w_gate_up, b_gate_up_l, b_gate_up_l)
    tn2 = min(1024, d)
    return pl.pallas_call(
        functools.partial(_gmm2_kernel, layer, tn2),
        grid_spec=pltpu.PrefetchScalarGridSpec(
            num_scalar_prefetch=3, grid=(d // tn2, n_tiles),
            in_specs=[
                pl.BlockSpec((MOE_TM, ff), lambda n, i, te, nx, na: (tile(i, na), 0)),
                pl.BlockSpec(memory_space=pl.ANY),
                pl.BlockSpec((None, 1, tn2), lambda n, i, te, nx, na: (te[i], 0, n)),
            ],
            out_specs=pl.BlockSpec((MOE_TM, tn2), lambda n, i, te, nx, na: (i, n)),
            scratch_shapes=[pltpu.VMEM((ff, tn2), BF16), pltpu.VMEM((ff, tn2), F32),
                            pltpu.SemaphoreType.DMA((1,))]),
        out_shape=jax.ShapeDtypeStruct((n_rows, d), F32),
        compiler_params=_params(("arbitrary", "arbitrary")),
        name="expert_down")(tile_expert, next_expert, n_active, act, w_down, b_down_l)


def _combine_ln_kernel(alpha, n_steps, dest_ref, h_ref, gate_ref, g_ref, b_ref, y_hbm, o_ref, ob_ref,
                       ybuf_ref, sem_ref):
    i = pl.program_id(0)
    slot = i % 2
    n_gather = TOP_K * COMBINE_TM

    @pl.when(i == 0)
    def _():
        _gather_rows(y_hbm, dest_ref, 0, n_gather, ybuf_ref.at[0], sem_ref.at[0])

    @pl.when(i + 1 < n_steps)
    def _():
        _gather_rows(y_hbm, dest_ref, (i + 1) * n_gather, n_gather, ybuf_ref.at[1 - slot], sem_ref.at[1 - slot])

    _wait_rows(y_hbm, n_gather, ybuf_ref.at[slot], sem_ref.at[slot])
    moe = None
    for k in range(TOP_K):
        term = gate_ref[:, k:k + 1] * ybuf_ref[slot, pl.ds(k * COMBINE_TM, COMBINE_TM), :]
        moe = term if moe is None else moe + term
    y = _ln_rows(alpha * h_ref[...] + moe, g_ref[...], b_ref[...])
    o_ref[...] = y
    ob_ref[...] = y.astype(BF16)


def _combine_ln(h, y_rows, dest_km, gate, g, b, alpha):
    rows, d = h.shape
    n_steps = rows // COMBINE_TM
    row_spec = pl.BlockSpec((COMBINE_TM, d), lambda i, dest: (i, 0))
    vec_spec = pl.BlockSpec((1, d), lambda i, dest: (0, 0))
    return pl.pallas_call(
        functools.partial(_combine_ln_kernel, alpha, n_steps),
        grid_spec=pltpu.PrefetchScalarGridSpec(
            num_scalar_prefetch=1, grid=(n_steps,),
            in_specs=[row_spec, pl.BlockSpec((COMBINE_TM, HEAD_W), lambda i, dest: (i, 0)),
                      vec_spec, vec_spec, pl.BlockSpec(memory_space=pl.ANY)],
            out_specs=(row_spec, row_spec),
            scratch_shapes=[pltpu.VMEM((2, TOP_K * COMBINE_TM, d), F32), pltpu.SemaphoreType.DMA((2,))]),
        out_shape=(jax.ShapeDtypeStruct((rows, d), F32), jax.ShapeDtypeStruct((rows, d), BF16)),
        compiler_params=_params(("arbitrary",)),
        name="combine_ln")(dest_km, h, gate, g.reshape(1, d), b.reshape(1, d), y_rows)


def _moe_ln(h, w_router_l, b_router_l, w_gate_up, b_gate_up_l, w_down, b_down_l, layer, g, b, alpha):
    rows, d = h.shape
    n_e = w_router_l.shape[-1]
    assert rows % COMBINE_TM == 0
    idx, gate, rank, counts = _router(h, w_router_l, b_router_l)
    idx, rank, counts = idx[:, :TOP_K], rank[:, :TOP_K], counts[0]
    padded = (counts + MOE_TM - 1) // MOE_TM * MOE_TM
    group_end = jnp.cumsum(padded)
    group_start = group_end - padded
    dest = group_start[idx] + rank
    n_tiles = (rows * TOP_K + n_e * (MOE_TM - 1)) // MOE_TM + 1
    n_rows = n_tiles * MOE_TM
    n_active = jnp.maximum(group_end[-1] // MOE_TM, 1).astype(I32)
    tile_ids = jnp.minimum(jnp.arange(n_tiles, dtype=I32), n_active - 1)
    tile_expert = jnp.sum((group_end[None, :] <= (tile_ids * MOE_TM)[:, None]).astype(I32), axis=1)
    tile_expert = jnp.minimum(tile_expert, n_e - 1)
    present = jnp.where(counts > 0, jnp.arange(n_e, dtype=I32), n_e)
    later = lax.cummin(present, axis=0, reverse=True)
    next_expert = jnp.concatenate([later[1:], jnp.full((1,), n_e, I32)])[tile_expert]
    tok = jnp.broadcast_to(jnp.arange(rows, dtype=I32)[:, None], (rows, TOP_K))
    row_tok = jnp.zeros((n_rows,), I32).at[dest.reshape(-1)].set(tok.reshape(-1))
    y_rows = _experts(h, row_tok, tile_expert, next_expert, n_active.reshape(1), w_gate_up,
                      b_gate_up_l.reshape(n_e, 1, -1), w_down, b_down_l.reshape(n_e, 1, -1), layer)
    dest_km = jnp.swapaxes(dest.reshape(rows // COMBINE_TM, COMBINE_TM, TOP_K), 1, 2).reshape(-1)
    return _combine_ln(h, y_rows, dest_km.astype(I32), gate, g, b, alpha)


def _rotary_tables(pos):
    half = A_HEAD_DIM // 2
    inv = ROPE_THETA ** (-jnp.arange(half, dtype=F32) / half)
    ang = pos.astype(F32)[:, None] * inv[None, :]
    cos, sin = jnp.cos(ang), jnp.sin(ang)
    reps = HEAD_W // A_HEAD_DIM
    return (jnp.concatenate([cos, cos] * reps, axis=1), jnp.concatenate([-sin, sin] * reps, axis=1))


def _sample_band_bias(rel_bias_l, q_pos, k_pos):
    rel = np.clip(q_pos[:, None] - k_pos[None, :], -REL_CLIP, REL_CLIP) + REL_CLIP
    qc, kc = q_pos[:, None] // CHUNK, k_pos[None, :] // CHUNK
    visible = (kc <= qc) & (kc >= qc - BAND_CHUNKS) & (k_pos[None, :] >= 0)
    return jnp.where(visible[None], rel_bias_l[:, rel], NEG_INF).astype(F32)


def kernel(x_prompt, x_sample, cache_a_k, cache_a_v, cache_b_k, cache_b_v, cache_c_k, cache_c_v, cache_c_logf, ln_in_g, ln_in_b, w_in, b_forget, diff_lambda, diff_subln_g, rel_bias, w_branch_a, w_branch_b, w_branch_c, w_gate, b_gate, w_out, ln_mix_g, ln_mix_b, w_router, b_router, w_gate_up, b_gate_up, w_down, b_down, ln_ffn_g, ln_ffn_b):
    n_p, t_p, d = x_prompt.shape
    n_s, t_s, _ = x_sample.shape
    depth, _, past = cache_a_k.shape[:3]
    band = cache_b_k.shape[2]
    assert n_p == 1 and past % CHUNK == 0 and t_s <= CHUNK and band == BAND_CHUNKS * CHUNK
    rows_s = n_s * t_s
    alpha = (2.0 * depth) ** 0.25

    x = jnp.concatenate([x_prompt.reshape(t_p, d), x_sample.reshape(rows_s, d)], axis=0)
    pos = jnp.concatenate([jnp.arange(t_p), jnp.tile(past + jnp.arange(t_s), n_s)])
    cos_t, sin_t = _rotary_tables(pos)
    q_pos_s = past + np.arange(t_s)
    k_pos_band = past - band + np.arange(band + t_s)
    pad_t = (-(past + t_s)) % HEAD_W

    h, hb = _layer_norm(x, ln_in_g, ln_in_b)
    states = []
    for l in range(depth):
        lam_init = 0.8 - 0.6 * math.exp(-0.3 * l)
        gain = (diff_subln_g[l] * (1.0 - lam_init)).reshape(1, HEAD_W)
        dl = diff_lambda[l]

        seg_f, seg_b = _project(hb, w_in, l, cos_t, sin_t)
        w_f = jnp.pad(w_in[l, :, N_SEG * WIDTH:], ((0, 0), (0, HEAD_W - N_HEADS)))
        b_f = jnp.pad(b_forget[l], (0, HEAD_W - N_HEADS)).reshape(1, HEAD_W)
        logf = _forget_logf(h, w_f, b_f)[:, :N_HEADS]
        logf_p, logf_s = logf[:t_p], logf[t_p:].reshape(n_s, t_s, N_HEADS)

        cum_p_t = _cumsum_lanes(logf_p.T)
        o_a_p = _prompt_flash("diff", seg_b, SEG_AQ, t_p, (dl, gain), lam_init)
        o_b_p = _prompt_band(seg_b, t_p, _band_bias_window(rel_bias[l]))
        o_c_p = _prompt_flash("forget", seg_b, SEG_CQ, t_p,
                              (cum_p_t.T, cum_p_t.reshape(N_HEADS, 1, t_p)))

        logf_all = jnp.concatenate([cache_c_logf[l], logf_s], axis=1)
        logf_all = jnp.pad(jnp.swapaxes(logf_all, 1, 2), ((0, 0), (0, 0), (0, pad_t)))
        cum_s = _cumsum_lanes(logf_all.reshape(n_s * N_HEADS, -1)).reshape(n_s, N_HEADS, -1)
        cum_new = cum_s[:, :, past:past + t_s]
        bias_s = _sample_band_bias(rel_bias[l], q_pos_s, k_pos_band)
        o_a_s = _sample_attn("diff", seg_b, SEG_AQ, cache_a_k, cache_a_v, l, t_p, (dl, gain), lam_init)
        o_b_s = _sample_attn("band", seg_b, SEG_BQ, cache_b_k, cache_b_v, l, t_p,
                             (bias_s[:, :, :band], bias_s[:, :, band:]))
        o_c_s = _sample_attn("forget", seg_b, SEG_CQ, cache_c_k, cache_c_v, l, t_p,
                             (jnp.swapaxes(cum_new, 1, 2), cum_s[:, :, :past], cum_new))

        o_a = jnp.concatenate([o_a_p, o_a_s], axis=0)
        o_b = jnp.concatenate([o_b_p, o_b_s], axis=0)
        o_c = jnp.concatenate([o_c_p, o_c_s], axis=0)
        mixed = _mix(hb, o_a, o_b, o_c, w_gate, b_gate[l].reshape(N_BRANCHES, d),
                     w_branch_a, w_branch_b, w_branch_c, l)
        h, hb = _layer_norm(_matmul(mixed, w_out, l), ln_mix_g[l], ln_mix_b[l], h, alpha)
        h, hb = _moe_ln(h, w_router[l], b_router[l], w_gate_up, b_gate_up[l], w_down, b_down[l], l,
                        ln_ffn_g[l], ln_ffn_b[l], alpha)
        states.append((seg_f, logf_p, logf_s))

    def prompt_state(seg, keep=None):
        out = jnp.stack([st[0][seg, :t_p] for st in states]).reshape(depth, 1, t_p, N_HEADS, HEAD_W)
        return out if keep is None else out[:, :, -keep:]

    def sample_state(seg):
        return jnp.stack([st[0][seg, t_p:] for st in states]).reshape(depth, n_s, t_s, N_HEADS, HEAD_W)

    band_p = min(BAND_CHUNKS * CHUNK, t_p)
    return (h[:t_p].reshape(1, t_p, d), h[t_p:].reshape(n_s, t_s, d),
            prompt_state(SEG_AK), prompt_state(SEG_AV),
            prompt_state(SEG_BK, band_p), prompt_state(SEG_BV, band_p),
            prompt_state(SEG_CK), prompt_state(SEG_CV),
            jnp.stack([st[1] for st in states]).reshape(depth, 1, t_p, N_HEADS),
            sample_state(SEG_AK), sample_state(SEG_AV), sample_state(SEG_BK), sample_state(SEG_BV),
            sample_state(SEG_CK), sample_state(SEG_CV),
            jnp.stack([st[2] for st in states]))
```

```python
import functools
import math

import numpy as np
import jax
import jax.numpy as jnp
from jax import lax
from jax.experimental import pallas as pl
from jax.experimental.pallas import tpu as pltpu

F32 = jnp.float32
BF16 = jnp.bfloat16
I32 = jnp.int32

CHUNK = 64
N_HEADS = 8
HEAD_W = 128
WIDTH = N_HEADS * HEAD_W
A_HEAD_DIM = 64
BAND_CHUNKS = 8
REL_CLIP = 128
N_BRANCHES = 3
N_SEG = 9
N_EXPERTS = 32
TOP_K = 4
SWIGLU_LIMIT = 7.0
SWIGLU_ALPHA = 1.702
ROPE_THETA = 10000.0
LN_EPS = 1e-5
NEG_INF = -1e30
VMEM_LIMIT_BYTES = 56 * 1024 * 1024

SEG_AQ, SEG_AK, SEG_AV, SEG_BQ, SEG_BK, SEG_BV, SEG_CQ, SEG_CK, SEG_CV = range(N_SEG)


def _params(sem):
    return pltpu.CompilerParams(dimension_semantics=sem, vmem_limit_bytes=VMEM_LIMIT_BYTES)


def _row_tile(rows, target, mult=16):
    best = None
    for t in range(mult, min(rows, target) + 1, mult):
        if rows % t == 0:
            best = t
    assert best is not None, (rows, target)
    return best


def _dot(a, b):
    return jnp.dot(a, b, preferred_element_type=F32)


def _dot_nt(a, b):
    return lax.dot_general(a, b, (((1,), (1,)), ((), ())), preferred_element_type=F32)


def _dot3(x, w):
    xh = x.astype(BF16)
    xl = (x - xh.astype(F32)).astype(BF16)
    wh = w.astype(BF16)
    wl = (w - wh.astype(F32)).astype(BF16)
    return _dot(xh, wh) + (_dot(xh, wl) + _dot(xl, wh))


def _ln_rows(x, g, b):
    mu = jnp.mean(x, axis=-1, keepdims=True)
    xc = x - mu
    var = jnp.mean(xc * xc, axis=-1, keepdims=True)
    return xc * lax.rsqrt(var + LN_EPS) * g + b


def _ln_kernel(x_ref, g_ref, b_ref, y_ref, yb_ref):
    y = _ln_rows(x_ref[...], g_ref[...], b_ref[...])
    y_ref[...] = y
    yb_ref[...] = y.astype(BF16)


def _add_ln_kernel(alpha, h_ref, z_ref, g_ref, b_ref, y_ref, yb_ref):
    y = _ln_rows(alpha * h_ref[...] + z_ref[...], g_ref[...], b_ref[...])
    y_ref[...] = y
    yb_ref[...] = y.astype(BF16)


def _layer_norm(x, g, b, h=None, alpha=None):
    rows, d = x.shape
    tm = _row_tile(rows, 512)
    row_spec = pl.BlockSpec((tm, d), lambda i: (i, 0))
    vec_spec = pl.BlockSpec((1, d), lambda i: (0, 0))
    g2, b2 = g.reshape(1, d), b.reshape(1, d)
    out_shape = (jax.ShapeDtypeStruct((rows, d), F32), jax.ShapeDtypeStruct((rows, d), BF16))
    if h is None:
        return pl.pallas_call(
            _ln_kernel, grid=(rows // tm,), in_specs=[row_spec, vec_spec, vec_spec],
            out_specs=(row_spec, row_spec), out_shape=out_shape,
            compiler_params=_params(("parallel",)), name="ln_in")(x, g2, b2)
    return pl.pallas_call(
        functools.partial(_add_ln_kernel, alpha), grid=(rows // tm,),
        in_specs=[row_spec, row_spec, vec_spec, vec_spec],
        out_specs=(row_spec, row_spec), out_shape=out_shape,
        compiler_params=_params(("parallel",)), name="add_ln")(h, x, g2, b2)


PROJ_CHUNKS = 2


def _proj_kernel(qscale, hb_ref, w_ref, cos_ref, sin_ref, f_ref, b_ref, wb_ref):
    j = pl.program_id(0)

    @pl.when(pl.program_id(1) == 0)
    def _():
        wb_ref[...] = w_ref[...].astype(BF16)

    cw = WIDTH // PROJ_CHUNKS

    def chunks(rotary, scale):
        for c in range(PROJ_CHUNKS):
            cols = slice(c * cw, (c + 1) * cw)
            y = _dot(hb_ref[...], wb_ref[:, cols])
            if rotary:
                cos = jnp.concatenate([cos_ref[...]] * (cw // HEAD_W), axis=1)
                sin = jnp.concatenate([sin_ref[...]] * (cw // HEAD_W), axis=1)
                lane = lax.broadcasted_iota(I32, y.shape, 1)
                first_half = (lane % A_HEAD_DIM) < (A_HEAD_DIM // 2)
                partner = jnp.where(first_half,
                                    pltpu.roll(y, cw - A_HEAD_DIM // 2, 1),
                                    pltpu.roll(y, A_HEAD_DIM // 2, 1))
                y = y * cos + partner * sin
            f_ref[:, cols] = y
            b_ref[:, cols] = (y * scale).astype(BF16)

    @pl.when(j <= SEG_AK)
    def _():
        chunks(True, jnp.where(j == SEG_AQ, qscale[SEG_AQ], 1.0))

    @pl.when(j > SEG_AK)
    def _():
        chunks(False, jnp.where((j == SEG_BQ) | (j == SEG_CQ), qscale[SEG_BQ], 1.0))


def _project(hb, w_in, layer, cos_t, sin_t):
    rows, d = hb.shape
    tm = _row_tile(rows, 640)
    qscale = {SEG_AQ: A_HEAD_DIM ** -0.5, SEG_BQ: HEAD_W ** -0.5}
    return pl.pallas_call(
        functools.partial(_proj_kernel, qscale),
        grid=(N_SEG, rows // tm),
        in_specs=[
            pl.BlockSpec((tm, d), lambda j, i: (i, 0)),
            pl.BlockSpec((None, d, WIDTH), lambda j, i: (layer, 0, j)),
            pl.BlockSpec((tm, HEAD_W), lambda j, i: (i, 0)),
            pl.BlockSpec((tm, HEAD_W), lambda j, i: (i, 0)),
        ],
        out_specs=(pl.BlockSpec((None, tm, WIDTH), lambda j, i: (j, i, 0)),
                   pl.BlockSpec((None, tm, WIDTH), lambda j, i: (j, i, 0))),
        out_shape=(jax.ShapeDtypeStruct((N_SEG, rows, WIDTH), F32),
                   jax.ShapeDtypeStruct((N_SEG, rows, WIDTH), BF16)),
        scratch_shapes=[pltpu.VMEM((d, WIDTH), BF16)],
        compiler_params=_params(("arbitrary", "arbitrary")), name="in_proj")(hb, w_in, cos_t, sin_t)


def _forget_kernel(h_ref, w_ref, bias_ref, o_ref):
    x = _dot3(h_ref[...], w_ref[...]) + bias_ref[...]
    o_ref[...] = jnp.minimum(x, 0.0) - jnp.log(1.0 + jnp.exp(-jnp.abs(x)))


def _forget_logf(h, w_f, b_f):
    rows, d = h.shape
    tm = _row_tile(rows, 512)
    return pl.pallas_call(
        _forget_kernel, grid=(rows // tm,),
        in_specs=[pl.BlockSpec((tm, d), lambda i: (i, 0)),
                  pl.BlockSpec((d, HEAD_W), lambda i: (0, 0)),
                  pl.BlockSpec((1, HEAD_W), lambda i: (0, 0))],
        out_specs=pl.BlockSpec((tm, HEAD_W), lambda i: (i, 0)),
        out_shape=jax.ShapeDtypeStruct((rows, HEAD_W), F32),
        compiler_params=_params(("parallel",)), name="forget_gate")(h, w_f, b_f)


def _cumsum_kernel(x_ref, o_ref):
    x = x_ref[...]
    n = x.shape[1]
    lane = lax.broadcasted_iota(I32, x.shape, 1)
    shift = 1
    while shift < n:
        x = x + jnp.where(lane >= shift, pltpu.roll(x, shift, 1), 0.0)
        shift *= 2
    o_ref[...] = x


def _cumsum_lanes(x):
    return pl.pallas_call(
        _cumsum_kernel, out_shape=jax.ShapeDtypeStruct(x.shape, F32),
        compiler_params=pltpu.CompilerParams(vmem_limit_bytes=VMEM_LIMIT_BYTES), name="cumsum")(x)


def _lanes(x, n):
    if n <= HEAD_W:
        return x[:, :n]
    return jnp.tile(x, (1, n // HEAD_W))


def _with_ones(v):
    return jnp.concatenate([v, jnp.ones_like(v)], axis=1)


def _softmax_block(s, v1, m_ref, acc_ref):
    m_old = m_ref[...]
    m_new = jnp.maximum(m_old, jnp.max(s, axis=-1, keepdims=True))
    p = jnp.exp(s - _lanes(m_new, s.shape[1]))
    a = jnp.exp(m_old - m_new)
    acc_ref[...] = _lanes(a, 2 * HEAD_W) * acc_ref[...] + _dot(p.astype(BF16), v1)
    m_ref[...] = m_new


def _init_stats(m_ref, acc_ref):
    m_ref[...] = jnp.full(m_ref.shape, NEG_INF, F32)
    acc_ref[...] = jnp.zeros(acc_ref.shape, F32)


def _normalized(acc):
    return acc[:, :HEAD_W] / acc[:, HEAD_W:]


def _split_halves(q):
    lane = lax.broadcasted_iota(I32, q.shape, 1)
    zero = jnp.zeros_like(q)
    return jnp.concatenate([jnp.where(lane < A_HEAD_DIM, q, zero),
                            jnp.where(lane >= A_HEAD_DIM, q, zero)], axis=0)


def _diff_lambda(dl, lam_init):
    s1 = jnp.sum(dl[0:1, :] * dl[1:2, :], axis=-1, keepdims=True)
    s2 = jnp.sum(dl[2:3, :] * dl[3:4, :], axis=-1, keepdims=True)
    return jnp.exp(s1) - jnp.exp(s2) + lam_init


def _diff_finish(acc, n, lam, gain):
    o = _normalized(acc[:n]) - lam * _normalized(acc[n:])
    return o * lax.rsqrt(jnp.mean(o * o, axis=-1, keepdims=True) + LN_EPS) * gain


FLASH_TQ = {"diff": 512, "forget": 1024}
FLASH_BIG = 2048


def _prompt_flash_kernel(kind, tq, big, lam_init, *refs):
    if kind == "diff":
        q_ref, k_ref, v_ref, dl_ref, gain_ref, o_ref, qs_ref, v1_ref, m_ref, acc_ref = refs
    else:
        q_ref, k_ref, v_ref, cq_ref, ck_ref, o_ref, qs_ref, v1_ref, cqr_ref, m_ref, acc_ref = refs
    h = pl.program_id(0)
    i = pl.program_id(1)
    q0 = i * tq
    nq = qs_ref.shape[0]

    @pl.when(i == 0)
    def _():
        v1_ref[...] = _with_ones(v_ref[...])

    if kind == "diff":
        qs_ref[...] = _split_halves(q_ref[...])
    else:
        qs_ref[...] = q_ref[...]
        head_lane = lax.broadcasted_iota(I32, cq_ref.shape, 1) == h
        cq = jnp.sum(jnp.where(head_lane, cq_ref[...], 0.0), axis=-1, keepdims=True)
        cqr_ref[...] = jnp.broadcast_to(cq, cqr_ref.shape)
    _init_stats(m_ref, acc_ref)

    half = nq // 2

    def block(k0, tk, diagonal):
        k = k_ref[pl.ds(k0, tk), :]
        v1 = v1_ref[pl.ds(k0, tk), :]
        scores = []
        for r in (0, half):
            s = _dot_nt(qs_ref[pl.ds(r, half), :], k)
            if kind == "forget":
                s = s + _lanes(cqr_ref[pl.ds(r, half), :], tk) - ck_ref[:, pl.ds(k0, tk)]
            if diagonal:
                qpos = q0 + (r + lax.broadcasted_iota(I32, (half, tk), 0)) % tq
                kpos = k0 + lax.broadcasted_iota(I32, (half, tk), 1)
                visible = (kpos // CHUNK) <= (qpos // CHUNK) if kind == "diff" else kpos <= qpos
                s = jnp.where(visible, s, NEG_INF)
            scores.append(s)
        for r, s in zip((0, half), scores):
            _softmax_block(s, v1, m_ref.at[pl.ds(r, half)], acc_ref.at[pl.ds(r, half)])

    def big_body(kb, carry):
        block(pl.multiple_of(kb * big, big), big, False)
        return carry

    def small_body(kb, carry):
        block(pl.multiple_of(kb * tq, tq), tq, False)
        return carry

    n_big = q0 // big
    lax.fori_loop(0, n_big, big_body, 0)
    lax.fori_loop(n_big * (big // tq), i, small_body, 0)
    block(pl.multiple_of(q0, tq), tq, True)

    if kind == "diff":
        lam = _diff_lambda(dl_ref[...], lam_init)
        o = _diff_finish(acc_ref[...], tq, lam, gain_ref[...])
    else:
        o = _normalized(acc_ref[...])
    o_ref[...] = o.astype(BF16)


def _prompt_flash(kind, qkv, seg_q, t, extra, lam_init=0.0):
    tq = min(FLASH_TQ[kind], t)
    big = min(FLASH_BIG, t)
    assert t % big == 0 and big % tq == 0
    nq = 2 * tq if kind == "diff" else tq
    in_specs = [
        pl.BlockSpec((None, tq, HEAD_W), lambda h, i: (seg_q, i, h)),
        pl.BlockSpec((None, t, HEAD_W), lambda h, i: (seg_q + 1, 0, h)),
        pl.BlockSpec((None, t, HEAD_W), lambda h, i: (seg_q + 2, 0, h)),
    ]
    scratch = [pltpu.VMEM((nq, HEAD_W), BF16), pltpu.VMEM((t, 2 * HEAD_W), BF16)]
    if kind == "diff":
        dl, gain = extra
        in_specs += [pl.BlockSpec(dl.shape, lambda h, i: (0, 0)),
                     pl.BlockSpec(gain.shape, lambda h, i: (0, 0))]
    else:
        cum, cum_t = extra
        in_specs += [pl.BlockSpec((tq, N_HEADS), lambda h, i: (i, 0)),
                     pl.BlockSpec((None, 1, t), lambda h, i: (h, 0, 0))]
        scratch += [pltpu.VMEM((tq, HEAD_W), F32)]
    scratch += [pltpu.VMEM((nq, HEAD_W), F32), pltpu.VMEM((nq, 2 * HEAD_W), F32)]
    return pl.pallas_call(
        functools.partial(_prompt_flash_kernel, kind, tq, big, lam_init),
        grid=(N_HEADS, t // tq), in_specs=in_specs,
        out_specs=pl.BlockSpec((tq, HEAD_W), lambda h, i: (i, h)),
        out_shape=jax.ShapeDtypeStruct((t, WIDTH), BF16),
        scratch_shapes=scratch,
        compiler_params=_params(("arbitrary", "arbitrary")),
        name="prompt_" + kind)(qkv, qkv, qkv, *extra)


BAND_TQ = 256
BAND_PAD = BAND_CHUNKS * CHUNK
BAND_WIN = BAND_PAD + BAND_TQ


def _prompt_band_kernel(q_ref, k_ref, v_ref, bias_ref, o_ref, kp_ref, v1_ref):
    i = pl.program_id(1)

    @pl.when(i == 0)
    def _():
        kp_ref[:BAND_PAD, :] = jnp.zeros((BAND_PAD, HEAD_W), BF16)
        kp_ref[BAND_PAD:, :] = k_ref[...]
        v1_ref[:BAND_PAD, :] = jnp.zeros((BAND_PAD, 2 * HEAD_W), BF16)
        v1_ref[BAND_PAD:, :] = _with_ones(v_ref[...])

    r0 = pl.multiple_of(i * BAND_TQ, BAND_TQ)
    k = kp_ref[pl.ds(r0, BAND_WIN), :]
    v1 = v1_ref[pl.ds(r0, BAND_WIN), :]
    half = BAND_TQ // 2
    in_range = r0 - BAND_PAD + lax.broadcasted_iota(I32, (half, BAND_WIN), 1) >= 0
    scores = [jnp.where(in_range, _dot_nt(q_ref[pl.ds(r, half), :], k) + bias_ref[pl.ds(r, half), :], NEG_INF)
              for r in (0, half)]
    for r, s in zip((0, half), scores):
        m = jnp.broadcast_to(jnp.max(s, axis=-1, keepdims=True), (half, HEAD_W))
        p = jnp.exp(s - _lanes(m, BAND_WIN))
        o_ref[pl.ds(r, half), :] = _normalized(_dot(p.astype(BF16), v1)).astype(BF16)


def _band_bias_window(rel_bias_l):
    n_tab = rel_bias_l.shape[0]
    span = BAND_TQ + BAND_WIN - 1
    dist = BAND_PAD + (BAND_TQ - 1) - np.arange(span)
    e = rel_bias_l[:, np.clip(dist, -REL_CLIP, REL_CLIP) + REL_CLIP]
    period = span + 1
    f = jnp.concatenate([e[:, BAND_TQ - 1:], jnp.zeros((n_tab, 1), e.dtype), e[:, :BAND_TQ - 1]], axis=1)
    skew = jnp.tile(f, (1, BAND_TQ))[:, :BAND_TQ * span].reshape(n_tab, BAND_TQ, span)
    bias = skew[:, :, :BAND_WIN]
    r = np.arange(BAND_TQ)[:, None]
    c = np.arange(BAND_WIN)[None, :] - BAND_PAD
    visible = (c // CHUNK <= r // CHUNK) & (c // CHUNK >= r // CHUNK - BAND_CHUNKS)
    assert period == span + 1
    return jnp.where(visible[None], bias, NEG_INF).astype(F32)


def _prompt_band(qkv, t, bias):
    assert t % BAND_TQ == 0 and BAND_TQ % CHUNK == 0
    return pl.pallas_call(
        _prompt_band_kernel, grid=(N_HEADS, t // BAND_TQ),
        in_specs=[
            pl.BlockSpec((None, BAND_TQ, HEAD_W), lambda h, i: (SEG_BQ, i, h)),
            pl.BlockSpec((None, t, HEAD_W), lambda h, i: (SEG_BK, 0, h)),
            pl.BlockSpec((None, t, HEAD_W), lambda h, i: (SEG_BV, 0, h)),
            pl.BlockSpec((None, BAND_TQ, BAND_WIN), lambda h, i: (h, 0, 0)),
        ],
        out_specs=pl.BlockSpec((BAND_TQ, HEAD_W), lambda h, i: (i, h)),
        out_shape=jax.ShapeDtypeStruct((t, WIDTH), BF16),
        scratch_shapes=[pltpu.VMEM((t + BAND_PAD, HEAD_W), BF16), pltpu.VMEM((t + BAND_PAD, 2 * HEAD_W), BF16)],
        compiler_params=_params(("arbitrary", "arbitrary")), name="prompt_band")(qkv, qkv, qkv, bias)


def _sample_attn_kernel(kind, layer, n_kt, tk, ts, lam_init, *refs):
    *refs, kbuf_ref, vbuf_ref, sem_ref = refs
    if kind == "diff":
        (q_ref, kc_hbm, vc_hbm, kn_ref, vn_ref, dl_ref, gain_ref, o_ref, qs_ref, m_ref, acc_ref) = refs
    elif kind == "band":
        (q_ref, kc_hbm, vc_hbm, kn_ref, vn_ref, bc_ref, bn_ref, o_ref, qs_ref, m_ref, acc_ref) = refs
    else:
        (q_ref, kc_hbm, vc_hbm, kn_ref, vn_ref, cq_ref, ckc_ref, ckn_ref, o_ref, qs_ref, m_ref, acc_ref) = refs
    kt = pl.program_id(1)
    nq = qs_ref.shape[1]
    step = pl.program_id(0) * n_kt + kt
    n_steps = pl.num_programs(0) * n_kt
    slot = step % 2

    def head_copies(stream, tile, buf):
        rows = pl.ds(pl.multiple_of(tile * tk, tk), tk)
        copies = []
        for h in range(N_HEADS):
            copies.append(pltpu.make_async_copy(kc_hbm.at[layer, stream, rows, h, :], kbuf_ref.at[buf, h],
                                                sem_ref.at[0, buf]))
            copies.append(pltpu.make_async_copy(vc_hbm.at[layer, stream, rows, h, :], vbuf_ref.at[buf, h],
                                                sem_ref.at[1, buf]))
        return copies

    @pl.when(step == 0)
    def _():
        for c in head_copies(0, 0, 0):
            c.start()

    @pl.when(step + 1 < n_steps)
    def _():
        for c in head_copies((step + 1) // n_kt, (step + 1) % n_kt, 1 - slot):
            c.start()

    for c in head_copies(pl.program_id(0), kt, slot):
        c.wait()

    @pl.when(kt == 0)
    def _():
        for h in range(N_HEADS):
            q = q_ref[:, h * HEAD_W:(h + 1) * HEAD_W]
            qs_ref[h] = _split_halves(q) if kind == "diff" else q
        _init_stats(m_ref, acc_ref)

    def head_bias(h, s, new):
        if kind == "band":
            return s + (bn_ref[h] if new else bc_ref[h])
        if kind == "forget":
            cq = cq_ref[:, h:h + 1]
            ck = ckn_ref[h:h + 1, :] if new else ckc_ref[h:h + 1, :]
            return s + cq - ck
        return s

    for h in range(N_HEADS):
        s = _dot_nt(qs_ref[h], kbuf_ref[slot, h].astype(BF16))
        _softmax_block(head_bias(h, s, False), _with_ones(vbuf_ref[slot, h].astype(BF16)),
                       m_ref.at[h], acc_ref.at[h])

    @pl.when(kt == n_kt - 1)
    def _():
        for h in range(N_HEADS):
            sl = slice(h * HEAD_W, (h + 1) * HEAD_W)
            s = head_bias(h, _dot_nt(qs_ref[h], kn_ref[:, sl]), True)
            if kind == "forget":
                r = lax.broadcasted_iota(I32, (nq, ts), 0)
                c = lax.broadcasted_iota(I32, (nq, ts), 1)
                s = jnp.where(c <= r, s, NEG_INF)
            _softmax_block(s, _with_ones(vn_ref[:, sl]), m_ref.at[h], acc_ref.at[h])
            if kind == "diff":
                lam = _diff_lambda(dl_ref[...], lam_init)
                o = _diff_finish(acc_ref[h], ts, lam, gain_ref[...])
            else:
                o = _normalized(acc_ref[h])
            o_ref[:, sl] = o.astype(BF16)


def _sample_attn(kind, qkv, seg_q, cache_k, cache_v, layer, t_prompt, extra, lam_init=0.0):
    _, n_streams, past = cache_k.shape[:3]
    rows = qkv.shape[1]
    ts = (rows - t_prompt) // n_streams
    assert t_prompt % ts == 0 and ts % 16 == 0
    tk = _row_tile(past, 1024, 128)
    n_kt = past // tk
    nq = 2 * ts if kind == "diff" else ts
    row0 = t_prompt // ts
    new_spec = lambda seg: pl.BlockSpec((None, ts, WIDTH), lambda b, kt: (seg, row0 + b, 0))
    cache_spec = pl.BlockSpec(memory_space=pl.ANY)
    in_specs = [new_spec(seg_q), cache_spec, cache_spec, new_spec(seg_q + 1), new_spec(seg_q + 2)]
    if kind == "diff":
        dl, gain = extra
        in_specs += [pl.BlockSpec(dl.shape, lambda b, kt: (0, 0)),
                     pl.BlockSpec(gain.shape, lambda b, kt: (0, 0))]
    elif kind == "band":
        bias_c, bias_n = extra
        in_specs += [pl.BlockSpec((N_HEADS, ts, tk), lambda b, kt: (0, 0, kt)),
                     pl.BlockSpec(bias_n.shape, lambda b, kt: (0, 0, 0))]
    else:
        cq, ck_cache, ck_new = extra
        in_specs += [pl.BlockSpec((None, ts, N_HEADS), lambda b, kt: (b, 0, 0)),
                     pl.BlockSpec((None, N_HEADS, tk), lambda b, kt: (b, 0, kt)),
                     pl.BlockSpec((None, N_HEADS, ts), lambda b, kt: (b, 0, 0))]
    return pl.pallas_call(
        functools.partial(_sample_attn_kernel, kind, layer, n_kt, tk, ts, lam_init),
        grid=(n_streams, n_kt), in_specs=in_specs,
        out_specs=pl.BlockSpec((ts, WIDTH), lambda b, kt: (b, 0)),
        out_shape=jax.ShapeDtypeStruct((n_streams * ts, WIDTH), BF16),
        scratch_shapes=[pltpu.VMEM((N_HEADS, nq, HEAD_W), BF16), pltpu.VMEM((N_HEADS, nq, HEAD_W), F32),
                        pltpu.VMEM((N_HEADS, nq, 2 * HEAD_W), F32),
                        pltpu.VMEM((2, N_HEADS, tk, HEAD_W), F32), pltpu.VMEM((2, N_HEADS, tk, HEAD_W), F32),
                        pltpu.SemaphoreType.DMA((2, 2))],
        compiler_params=_params(("arbitrary", "arbitrary")),
        name="sample_" + kind)(qkv, cache_k, cache_v, qkv, qkv, *extra)


def _mix_kernel(hb_ref, oa_ref, ob_ref, oc_ref, wg0_ref, wg1_ref, wg2_ref, bg_ref,
                wa_ref, wb_ref, wc_ref, o_ref, wgs_ref, wbs_ref):
    @pl.when(pl.program_id(1) == 0)
    def _():
        for b, (wg, wbr) in enumerate(((wg0_ref, wa_ref), (wg1_ref, wb_ref), (wg2_ref, wc_ref))):
            wgs_ref[b] = wg[...].astype(BF16)
            wbs_ref[b] = wbr[...].astype(BF16)

    hb = hb_ref[...]
    mixed = None
    for b, o in enumerate((oa_ref, ob_ref, oc_ref)):
        gate = jax.nn.sigmoid(_dot(hb, wgs_ref[b]) + bg_ref[b:b + 1, :])
        term = gate * _dot(o[...], wbs_ref[b])
        mixed = term if mixed is None else mixed + term
    o_ref[...] = mixed.astype(BF16)


def _mix(hb, o_a, o_b, o_c, w_gate, b_gate_l, w_a, w_b, w_c, layer):
    rows, d = hb.shape
    tm = _row_tile(rows, 640)
    tn = 256
    nb = d // tn
    gate_spec = lambda b: pl.BlockSpec((None, d, tn), lambda n, i: (layer, 0, b * nb + n))
    branch_spec = pl.BlockSpec((None, WIDTH, tn), lambda n, i: (layer, 0, n))
    o_spec = pl.BlockSpec((tm, WIDTH), lambda n, i: (i, 0))
    return pl.pallas_call(
        _mix_kernel, grid=(nb, rows // tm),
        in_specs=[pl.BlockSpec((tm, d), lambda n, i: (i, 0)), o_spec, o_spec, o_spec,
                  gate_spec(0), gate_spec(1), gate_spec(2),
                  pl.BlockSpec((N_BRANCHES, tn), lambda n, i: (0, n)),
                  branch_spec, branch_spec, branch_spec],
        out_specs=pl.BlockSpec((tm, tn), lambda n, i: (i, n)),
        out_shape=jax.ShapeDtypeStruct((rows, d), BF16),
        scratch_shapes=[pltpu.VMEM((N_BRANCHES, d, tn), BF16), pltpu.VMEM((N_BRANCHES, WIDTH, tn), BF16)],
        compiler_params=_params(("arbitrary", "arbitrary")),
        name="branch_mix")(hb, o_a, o_b, o_c, w_gate, w_gate, w_gate, b_gate_l, w_a, w_b, w_c)


def _matmul_kernel(x_ref, w_ref, o_ref, wb_ref):
    @pl.when(pl.program_id(1) == 0)
    def _():
        wb_ref[...] = w_ref[...].astype(BF16)
    o_ref[...] = _dot(x_ref[...], wb_ref[...])


def _matmul(x, w, layer):
    rows, k = x.shape
    n = w.shape[-1]
    tm = _row_tile(rows, 640)
    tn = min(512, n)
    assert n % tn == 0
    return pl.pallas_call(
        _matmul_kernel, grid=(n // tn, rows // tm),
        in_specs=[pl.BlockSpec((tm, k), lambda j, i: (i, 0)),
                  pl.BlockSpec((None, k, tn), lambda j, i: (layer, 0, j))],
        out_specs=pl.BlockSpec((tm, tn), lambda j, i: (i, j)),
        out_shape=jax.ShapeDtypeStruct((rows, n), F32),
        scratch_shapes=[pltpu.VMEM((k, tn), BF16)],
        compiler_params=_params(("arbitrary", "arbitrary")), name="out_proj")(x, w)


def _router_kernel(h_ref, w_ref, b_ref, idx_ref, gate_ref, rank_ref, cnt_ref, run_ref):
    @pl.when(pl.program_id(0) == 0)
    def _():
        run_ref[...] = jnp.zeros(run_ref.shape, F32)

    logits = _dot3(h_ref[...], w_ref[...]) + b_ref[...]
    tm, n_e = logits.shape
    e_iota = lax.broadcasted_iota(I32, (tm, n_e), 1)
    out_lane = lax.broadcasted_iota(I32, (tm, HEAD_W), 1)
    vals, hots, idxs = [], [], []
    for _ in range(TOP_K):
        top = jnp.max(logits, axis=-1, keepdims=True)
        idx = jnp.min(jnp.where(logits == top, e_iota, n_e), axis=-1, keepdims=True)
        hot = e_iota == idx
        vals.append(top)
        idxs.append(idx)
        hots.append(hot)
        logits = jnp.where(hot, -jnp.inf, logits)
    exps = [jnp.exp(v - vals[0]) for v in vals]
    denom = exps[0] + exps[1] + exps[2] + exps[3]

    chosen = hots[0] | hots[1] | hots[2] | hots[3]
    chosen_f = jnp.where(chosen, 1.0, 0.0)
    r = lax.broadcasted_iota(I32, (tm, tm), 0)
    c = lax.broadcasted_iota(I32, (tm, tm), 1)
    before = jnp.where(c < r, 1.0, 0.0).astype(BF16)
    rank = _dot(before, chosen_f.astype(BF16)) + run_ref[...]
    run_ref[...] = run_ref[...] + jnp.sum(chosen_f, axis=0, keepdims=True)

    idx_out = jnp.zeros((tm, HEAD_W), I32)
    gate_out = jnp.zeros((tm, HEAD_W), F32)
    rank_out = jnp.zeros((tm, HEAD_W), I32)
    for k in range(TOP_K):
        rk = jnp.sum(jnp.where(hots[k], rank, 0.0), axis=-1, keepdims=True).astype(I32)
        idx_out = jnp.where(out_lane == k, idxs[k], idx_out)
        gate_out = jnp.where(out_lane == k, exps[k] / denom, gate_out)
        rank_out = jnp.where(out_lane == k, rk, rank_out)
    idx_ref[...] = idx_out
    gate_ref[...] = gate_out
    rank_ref[...] = rank_out
    cnt_ref[...] = run_ref[...].astype(I32)


def _router(h, w_router_l, b_router_l):
    rows, d = h.shape
    n_e = w_router_l.shape[-1]
    tm = _row_tile(rows, 256)
    lanes = pl.BlockSpec((tm, HEAD_W), lambda i: (i, 0))
    return pl.pallas_call(
        _router_kernel, grid=(rows // tm,),
        in_specs=[pl.BlockSpec((tm, d), lambda i: (i, 0)),
                  pl.BlockSpec((d, n_e), lambda i: (0, 0)),
                  pl.BlockSpec((1, n_e), lambda i: (0, 0))],
        out_specs=(lanes, lanes, lanes, pl.BlockSpec((1, n_e), lambda i: (0, 0))),
        out_shape=(jax.ShapeDtypeStruct((rows, HEAD_W), I32), jax.ShapeDtypeStruct((rows, HEAD_W), F32),
                   jax.ShapeDtypeStruct((rows, HEAD_W), I32), jax.ShapeDtypeStruct((1, n_e), I32)),
        scratch_shapes=[pltpu.VMEM((1, n_e), F32)],
        compiler_params=_params(("arbitrary",)), name="router")(h, w_router_l, b_router_l.reshape(1, n_e))


MOE_TM = 256
COMBINE_TM = 128


def _expert_changed(te_ref, i):
    return (i == 0) | (te_ref[i] != te_ref[jnp.maximum(i - 1, 0)])


def _gather_rows(src_hbm, idx_ref, base, n_rows, dst, sem):
    for r in range(n_rows):
        pltpu.make_async_copy(src_hbm.at[pl.ds(idx_ref[base + r], 1)], dst.at[pl.ds(r, 1)], sem).start()


def _wait_rows(src_hbm, n_rows, dst, sem):
    pltpu.make_async_copy(src_hbm.at[pl.ds(0, n_rows)], dst, sem).wait()


def _weight_copy(w_hbm, layer, expert, col0, tn, stage_ref, sem):
    return pltpu.make_async_copy(w_hbm.at[layer, expert, :, pl.ds(col0, tn)], stage_ref, sem)


def _staged_weights(copies_for, te_ref, nx_ref, n, i, live, cast):
    @pl.when((n == 0) & (i == 0))
    def _():
        for c in copies_for(te_ref[0], 0):
            c.start()

    @pl.when(live & _expert_changed(te_ref, i))
    def _():
        for c in copies_for(te_ref[i], n):
            c.wait()
        cast()
        nxt = nx_ref[i]
        more = nxt < N_EXPERTS

        @pl.when(more)
        def _():
            for c in copies_for(nxt, n):
                c.start()

        @pl.when(jnp.logical_not(more) & (n + 1 < pl.num_programs(0)))
        def _():
            for c in copies_for(te_ref[0], n + 1):
                c.start()


def _gmm1_kernel(layer, tn, ff, te_ref, nx_ref, na_ref, tok_ref, h_hbm, w_hbm, bg_ref, bu_ref, o_ref,
                 wgs_ref, wus_ref, stg_ref, stu_ref, xbuf_ref, xsem_ref, wsem_ref):
    n = pl.program_id(0)
    i = pl.program_id(1)
    n_active = na_ref[0]
    live = i < n_active
    slot = i % 2

    def copies_for(expert, sweep):
        c0 = pl.multiple_of(sweep * tn, tn)
        return (_weight_copy(w_hbm, layer, expert, c0, tn, stg_ref, wsem_ref.at[0]),
                _weight_copy(w_hbm, layer, expert, ff + c0, tn, stu_ref, wsem_ref.at[1]))

    def cast():
        wgs_ref[...] = stg_ref[...].astype(BF16)
        wus_ref[...] = stu_ref[...].astype(BF16)

    @pl.when(i == 0)
    def _():
        _gather_rows(h_hbm, tok_ref, 0, MOE_TM, xbuf_ref.at[0], xsem_ref.at[0])

    _staged_weights(copies_for, te_ref, nx_ref, n, i, live, cast)

    @pl.when(live)
    def _():
        _wait_rows(h_hbm, MOE_TM, xbuf_ref.at[slot], xsem_ref.at[slot])
        ahead = jnp.minimum(i + 1, n_active - 1)
        _gather_rows(h_hbm, tok_ref, ahead * MOE_TM, MOE_TM, xbuf_ref.at[1 - slot], xsem_ref.at[1 - slot])
        x = xbuf_ref[slot].astype(BF16)
        g = jnp.minimum(_dot(x, wgs_ref[...]) + bg_ref[...], SWIGLU_LIMIT)
        u = jnp.clip(_dot(x, wus_ref[...]) + bu_ref[...], -SWIGLU_LIMIT, SWIGLU_LIMIT)
        o_ref[...] = (g * jax.nn.sigmoid(SWIGLU_ALPHA * g) * (u + 1.0)).astype(BF16)

    @pl.when(i == n_active - 1)
    def _():
        _wait_rows(h_hbm, MOE_TM, xbuf_ref.at[1 - slot], xsem_ref.at[1 - slot])

    @pl.when(jnp.logical_not(live))
    def _():
        o_ref[...] = jnp.zeros(o_ref.shape, o_ref.dtype)


def _gmm2_kernel(layer, tn, te_ref, nx_ref, na_ref, x_ref, w_hbm, b_ref, o_ref, ws_ref, st_ref, wsem_ref):
    n = pl.program_id(0)
    i = pl.program_id(1)
    live = i < na_ref[0]

    def copies_for(expert, sweep):
        return (_weight_copy(w_hbm, layer, expert, pl.multiple_of(sweep * tn, tn), tn, st_ref, wsem_ref.at[0]),)

    def cast():
        ws_ref[...] = st_ref[...].astype(BF16)

    _staged_weights(copies_for, te_ref, nx_ref, n, i, live, cast)

    @pl.when(live)
    def _():
        o_ref[...] = _dot(x_ref[...], ws_ref[...]) + b_ref[...]

    @pl.when(jnp.logical_not(live))
    def _():
        o_ref[...] = jnp.zeros(o_ref.shape, o_ref.dtype)


def _experts(h, row_tok, tile_expert, next_expert, n_active, w_gate_up, b_gate_up_l, w_down, b_down_l, layer):
    n_rows = row_tok.shape[0]
    d = h.shape[1]
    ff = w_down.shape[2]
    n_tiles = n_rows // MOE_TM
    tn = min(1024, ff)
    nff = ff // tn
    tile = lambda i, na: jnp.minimum(i, na[0] - 1)
    act = pl.pallas_call(
        functools.partial(_gmm1_kernel, layer, tn, ff),
        grid_spec=pltpu.PrefetchScalarGridSpec(
            num_scalar_prefetch=4, grid=(nff, n_tiles),
            in_specs=[
                pl.BlockSpec(memory_space=pl.ANY),
                pl.BlockSpec(memory_space=pl.ANY),
                pl.BlockSpec((None, 1, tn), lambda n, i, te, nx, na, tok: (te[i], 0, n)),
                pl.BlockSpec((None, 1, tn), lambda n, i, te, nx, na, tok: (te[i], 0, nff + n)),
            ],
            out_specs=pl.BlockSpec((MOE_TM, tn), lambda n, i, te, nx, na, tok: (i, n)),
            scratch_shapes=[pltpu.VMEM((d, tn), BF16), pltpu.VMEM((d, tn), BF16),
                            pltpu.VMEM((d, tn), F32), pltpu.VMEM((d, tn), F32),
                            pltpu.VMEM((2, MOE_TM, d), F32),
                            pltpu.SemaphoreType.DMA((2,)), pltpu.SemaphoreType.DMA((2,))]),
        out_shape=jax.ShapeDtypeStruct((n_rows, ff), BF16),
        compiler_params=_params(("arbitrary", "arbitrary")),
        name="expert_up")(tile_expert, next_expert, n_active, row_tok, h, w_gate_up, b_gate_up_l, b_gate_up_l)
    tn2 = min(1024, d)
    return pl.pallas_call(
        functools.partial(_gmm2_kernel, layer, tn2),
        grid_spec=pltpu.PrefetchScalarGridSpec(
            num_scalar_prefetch=3, grid=(d // tn2, n_tiles),
            in_specs=[
                pl.BlockSpec((MOE_TM, ff), lambda n, i, te, nx, na: (tile(i, na), 0)),
                pl.BlockSpec(memory_space=pl.ANY),
                pl.BlockSpec((None, 1, tn2), lambda n, i, te, nx, na: (te[i], 0, n)),
            ],
            out_specs=pl.BlockSpec((MOE_TM, tn2), lambda n, i, te, nx, na: (i, n)),
            scratch_shapes=[pltpu.VMEM((ff, tn2), BF16), pltpu.VMEM((ff, tn2), F32),
                            pltpu.SemaphoreType.DMA((1,))]),
        out_shape=jax.ShapeDtypeStruct((n_rows, d), F32),
        compiler_params=_params(("arbitrary", "arbitrary")),
        name="expert_down")(tile_expert, next_expert, n_active, act, w_down, b_down_l)


def _combine_ln_kernel(alpha, n_steps, dest_ref, h_ref, gate_ref, g_ref, b_ref, y_hbm, o_ref, ob_ref,
                       ybuf_ref, sem_ref):
    i = pl.program_id(0)
    slot = i % 2
    n_gather = TOP_K * COMBINE_TM

    @pl.when(i == 0)
    def _():
        _gather_rows(y_hbm, dest_ref, 0, n_gather, ybuf_ref.at[0], sem_ref.at[0])

    @pl.when(i + 1 < n_steps)
    def _():
        _gather_rows(y_hbm, dest_ref, (i + 1) * n_gather, n_gather, ybuf_ref.at[1 - slot], sem_ref.at[1 - slot])

    _wait_rows(y_hbm, n_gather, ybuf_ref.at[slot], sem_ref.at[slot])
    moe = None
    for k in range(TOP_K):
        term = gate_ref[:, k:k + 1] * ybuf_ref[slot, pl.ds(k * COMBINE_TM, COMBINE_TM), :]
        moe = term if moe is None else moe + term
    y = _ln_rows(alpha * h_ref[...] + moe, g_ref[...], b_ref[...])
    o_ref[...] = y
    ob_ref[...] = y.astype(BF16)


def _combine_ln(h, y_rows, dest_km, gate, g, b, alpha):
    rows, d = h.shape
    n_steps = rows // COMBINE_TM
    row_spec = pl.BlockSpec((COMBINE_TM, d), lambda i, dest: (i, 0))
    vec_spec = pl.BlockSpec((1, d), lambda i, dest: (0, 0))
    return pl.pallas_call(
        functools.partial(_combine_ln_kernel, alpha, n_steps),
        grid_spec=pltpu.PrefetchScalarGridSpec(
            num_scalar_prefetch=1, grid=(n_steps,),
            in_specs=[row_spec, pl.BlockSpec((COMBINE_TM, HEAD_W), lambda i, dest: (i, 0)),
                      vec_spec, vec_spec, pl.BlockSpec(memory_space=pl.ANY)],
            out_specs=(row_spec, row_spec),
            scratch_shapes=[pltpu.VMEM((2, TOP_K * COMBINE_TM, d), F32), pltpu.SemaphoreType.DMA((2,))]),
        out_shape=(jax.ShapeDtypeStruct((rows, d), F32), jax.ShapeDtypeStruct((rows, d), BF16)),
        compiler_params=_params(("arbitrary",)),
        name="combine_ln")(dest_km, h, gate, g.reshape(1, d), b.reshape(1, d), y_rows)


def _moe_ln(h, w_router_l, b_router_l, w_gate_up, b_gate_up_l, w_down, b_down_l, layer, g, b, alpha):
    rows, d = h.shape
    n_e = w_router_l.shape[-1]
    assert rows % COMBINE_TM == 0
    idx, gate, rank, counts = _router(h, w_router_l, b_router_l)
    idx, rank, counts = idx[:, :TOP_K], rank[:, :TOP_K], counts[0]
    padded = (counts + MOE_TM - 1) // MOE_TM * MOE_TM
    group_end = jnp.cumsum(padded)
    group_start = group_end - padded
    dest = group_start[idx] + rank
    n_tiles = (rows * TOP_K + n_e * (MOE_TM - 1)) // MOE_TM + 1
    n_rows = n_tiles * MOE_TM
    n_active = jnp.maximum(group_end[-1] // MOE_TM, 1).astype(I32)
    tile_ids = jnp.minimum(jnp.arange(n_tiles, dtype=I32), n_active - 1)
    tile_expert = jnp.sum((group_end[None, :] <= (tile_ids * MOE_TM)[:, None]).astype(I32), axis=1)
    tile_expert = jnp.minimum(tile_expert, n_e - 1)
    present = jnp.where(counts > 0, jnp.arange(n_e, dtype=I32), n_e)
    later = lax.cummin(present, axis=0, reverse=True)
    next_expert = jnp.concatenate([later[1:], jnp.full((1,), n_e, I32)])[tile_expert]
    tok = jnp.broadcast_to(jnp.arange(rows, dtype=I32)[:, None], (rows, TOP_K))
    row_tok = jnp.zeros((n_rows,), I32).at[dest.reshape(-1)].set(tok.reshape(-1))
    y_rows = _experts(h, row_tok, tile_expert, next_expert, n_active.reshape(1), w_gate_up,
                      b_gate_up_l.reshape(n_e, 1, -1), w_down, b_down_l.reshape(n_e, 1, -1), layer)
    dest_km = jnp.swapaxes(dest.reshape(rows // COMBINE_TM, COMBINE_TM, TOP_K), 1, 2).reshape(-1)
    return _combine_ln(h, y_rows, dest_km.astype(I32), gate, g, b, alpha)


def _rotary_tables(pos):
    half = A_HEAD_DIM // 2
    inv = ROPE_THETA ** (-jnp.arange(half, dtype=F32) / half)
    ang = pos.astype(F32)[:, None] * inv[None, :]
    cos, sin = jnp.cos(ang), jnp.sin(ang)
    reps = HEAD_W // A_HEAD_DIM
    return (jnp.concatenate([cos, cos] * reps, axis=1), jnp.concatenate([-sin, sin] * reps, axis=1))


def _sample_band_bias(rel_bias_l, q_pos, k_pos):
    rel = np.clip(q_pos[:, None] - k_pos[None, :], -REL_CLIP, REL_CLIP) + REL_CLIP
    qc, kc = q_pos[:, None] // CHUNK, k_pos[None, :] // CHUNK
    visible = (kc <= qc) & (kc >= qc - BAND_CHUNKS) & (k_pos[None, :] >= 0)
    return jnp.where(visible[None], rel_bias_l[:, rel], NEG_INF).astype(F32)


def kernel(x_prompt, x_sample, cache_a_k, cache_a_v, cache_b_k, cache_b_v, cache_c_k, cache_c_v, cache_c_logf, ln_in_g, ln_in_b, w_in, b_forget, diff_lambda, diff_subln_g, rel_bias, w_branch_a, w_branch_b, w_branch_c, w_gate, b_gate, w_out, ln_mix_g, ln_mix_b, w_router, b_router, w_gate_up, b_gate_up, w_down, b_down, ln_ffn_g, ln_ffn_b):
    n_p, t_p, d = x_prompt.shape
    n_s, t_s, _ = x_sample.shape
    depth, _, past = cache_a_k.shape[:3]
    band = cache_b_k.shape[2]
    assert n_p == 1 and past % CHUNK == 0 and t_s <= CHUNK and band == BAND_CHUNKS * CHUNK
    rows_s = n_s * t_s
    alpha = (2.0 * depth) ** 0.25

    x = jnp.concatenate([x_prompt.reshape(t_p, d), x_sample.reshape(rows_s, d)], axis=0)
    pos = jnp.concatenate([jnp.arange(t_p), jnp.tile(past + jnp.arange(t_s), n_s)])
    cos_t, sin_t = _rotary_tables(pos)
    q_pos_s = past + np.arange(t_s)
    k_pos_band = past - band + np.arange(band + t_s)
    pad_t = (-(past + t_s)) % HEAD_W

    h, hb = _layer_norm(x, ln_in_g, ln_in_b)
    states = []
    for l in range(depth):
        lam_init = 0.8 - 0.6 * math.exp(-0.3 * l)
        gain = (diff_subln_g[l] * (1.0 - lam_init)).reshape(1, HEAD_W)
        dl = diff_lambda[l]

        seg_f, seg_b = _project(hb, w_in, l, cos_t, sin_t)
        w_f = jnp.pad(w_in[l, :, N_SEG * WIDTH:], ((0, 0), (0, HEAD_W - N_HEADS)))
        b_f = jnp.pad(b_forget[l], (0, HEAD_W - N_HEADS)).reshape(1, HEAD_W)
        logf = _forget_logf(h, w_f, b_f)[:, :N_HEADS]
        logf_p, logf_s = logf[:t_p], logf[t_p:].reshape(n_s, t_s, N_HEADS)

        cum_p_t = _cumsum_lanes(logf_p.T)
        o_a_p = _prompt_flash("diff", seg_b, SEG_AQ, t_p, (dl, gain), lam_init)
        o_b_p = _prompt_band(seg_b, t_p, _band_bias_window(rel_bias[l]))
        o_c_p = _prompt_flash("forget", seg_b, SEG_CQ, t_p,
                              (cum_p_t.T, cum_p_t.reshape(N_HEADS, 1, t_p)))

        logf_all = jnp.concatenate([cache_c_logf[l], logf_s], axis=1)
        logf_all = jnp.pad(jnp.swapaxes(logf_all, 1, 2), ((0, 0), (0, 0), (0, pad_t)))
        cum_s = _cumsum_lanes(logf_all.reshape(n_s * N_HEADS, -1)).reshape(n_s, N_HEADS, -1)
        cum_new = cum_s[:, :, past:past + t_s]
        bias_s = _sample_band_bias(rel_bias[l], q_pos_s, k_pos_band)
        o_a_s = _sample_attn("diff", seg_b, SEG_AQ, cache_a_k, cache_a_v, l, t_p, (dl, gain), lam_init)
        o_b_s = _sample_attn("band", seg_b, SEG_BQ, cache_b_k, cache_b_v, l, t_p,
                             (bias_s[:, :, :band], bias_s[:, :, band:]))
        o_c_s = _sample_attn("forget", seg_b, SEG_CQ, cache_c_k, cache_c_v, l, t_p,
                             (jnp.swapaxes(cum_new, 1, 2), cum_s[:, :, :past], cum_new))

        o_a = jnp.concatenate([o_a_p, o_a_s], axis=0)
        o_b = jnp.concatenate([o_b_p, o_b_s], axis=0)
        o_c = jnp.concatenate([o_c_p, o_c_s], axis=0)
        mixed = _mix(hb, o_a, o_b, o_c, w_gate, b_gate[l].reshape(N_BRANCHES, d),
                     w_branch_a, w_branch_b, w_branch_c, l)
        h, hb = _layer_norm(_matmul(mixed, w_out, l), ln_mix_g[l], ln_mix_b[l], h, alpha)
        h, hb = _moe_ln(h, w_router[l], b_router[l], w_gate_up, b_gate_up[l], w_down, b_down[l], l,
                        ln_ffn_g[l], ln_ffn_b[l], alpha)
        states.append((seg_f, logf_p, logf_s))

    def prompt_state(seg, keep=None):
        out = jnp.stack([st[0][seg, :t_p] for st in states]).reshape(depth, 1, t_p, N_HEADS, HEAD_W)
        return out if keep is None else out[:, :, -keep:]

    def sample_state(seg):
        return jnp.stack([st[0][seg, t_p:] for st in states]).reshape(depth, n_s, t_s, N_HEADS, HEAD_W)

    band_p = min(BAND_CHUNKS * CHUNK, t_p)
    return (h[:t_p].reshape(1, t_p, d), h[t_p:].reshape(n_s, t_s, d),
            prompt_state(SEG_AK), prompt_state(SEG_AV),
            prompt_state(SEG_BK, band_p), prompt_state(SEG_BV, band_p),
            prompt_state(SEG_CK), prompt_state(SEG_CV),
            jnp.stack([st[1] for st in states]).reshape(depth, 1, t_p, N_HEADS),
            sample_state(SEG_AK), sample_state(SEG_AV), sample_state(SEG_BK), sample_state(SEG_BV),
            sample_state(SEG_CK), sample_state(SEG_CV),
            jnp.stack([st[2] for st in states]))
```

```python
import functools
import math

import numpy as np
import jax
import jax.numpy as jnp
from jax import lax
from jax.experimental import pallas as pl
from jax.experimental.pallas import tpu as pltpu

F32 = jnp.float32
BF16 = jnp.bfloat16
I32 = jnp.int32

CHUNK = 64
N_HEADS = 8
HEAD_W = 128
WIDTH = N_HEADS * HEAD_W
A_HEAD_DIM = 64
BAND_CHUNKS = 8
REL_CLIP = 128
N_BRANCHES = 3
N_SEG = 9
N_EXPERTS = 32
TOP_K = 4
SWIGLU_LIMIT = 7.0
SWIGLU_ALPHA = 1.702
ROPE_THETA = 10000.0
LN_EPS = 1e-5
NEG_INF = -1e30
VMEM_LIMIT_BYTES = 56 * 1024 * 1024

SEG_AQ, SEG_AK, SEG_AV, SEG_BQ, SEG_BK, SEG_BV, SEG_CQ, SEG_CK, SEG_CV = range(N_SEG)


def _params(sem):
    return pltpu.CompilerParams(dimension_semantics=sem, vmem_limit_bytes=VMEM_LIMIT_BYTES)


def _row_tile(rows, target, mult=16):
    best = None
    for t in range(mult, min(rows, target) + 1, mult):
        if rows % t == 0:
            best = t
    assert best is not None, (rows, target)
    return best


def _dot(a, b):
    return jnp.dot(a, b, preferred_element_type=F32)


def _dot_nt(a, b):
    return lax.dot_general(a, b, (((1,), (1,)), ((), ())), preferred_element_type=F32)


def _dot3(x, w):
    xh = x.astype(BF16)
    xl = (x - xh.astype(F32)).astype(BF16)
    wh = w.astype(BF16)
    wl = (w - wh.astype(F32)).astype(BF16)
    return _dot(xh, wh) + (_dot(xh, wl) + _dot(xl, wh))


def _ln_rows(x, g, b):
    mu = jnp.mean(x, axis=-1, keepdims=True)
    xc = x - mu
    var = jnp.mean(xc * xc, axis=-1, keepdims=True)
    return xc * lax.rsqrt(var + LN_EPS) * g + b


def _ln_kernel(x_ref, g_ref, b_ref, y_ref, yb_ref):
    y = _ln_rows(x_ref[...], g_ref[...], b_ref[...])
    y_ref[...] = y
    yb_ref[...] = y.astype(BF16)


def _add_ln_kernel(alpha, h_ref, z_ref, g_ref, b_ref, y_ref, yb_ref):
    y = _ln_rows(alpha * h_ref[...] + z_ref[...], g_ref[...], b_ref[...])
    y_ref[...] = y
    yb_ref[...] = y.astype(BF16)


def _layer_norm(x, g, b, h=None, alpha=None):
    rows, d = x.shape
    tm = _row_tile(rows, 512)
    row_spec = pl.BlockSpec((tm, d), lambda i: (i, 0))
    vec_spec = pl.BlockSpec((1, d), lambda i: (0, 0))
    g2, b2 = g.reshape(1, d), b.reshape(1, d)
    out_shape = (jax.ShapeDtypeStruct((rows, d), F32), jax.ShapeDtypeStruct((rows, d), BF16))
    if h is None:
        return pl.pallas_call(
            _ln_kernel, grid=(rows // tm,), in_specs=[row_spec, vec_spec, vec_spec],
            out_specs=(row_spec, row_spec), out_shape=out_shape,
            compiler_params=_params(("parallel",)), name="ln_in")(x, g2, b2)
    return pl.pallas_call(
        functools.partial(_add_ln_kernel, alpha), grid=(rows // tm,),
        in_specs=[row_spec, row_spec, vec_spec, vec_spec],
        out_specs=(row_spec, row_spec), out_shape=out_shape,
        compiler_params=_params(("parallel",)), name="add_ln")(h, x, g2, b2)


PROJ_CHUNKS = 2


def _proj_kernel(qscale, hb_ref, w_ref, cos_ref, sin_ref, f_ref, b_ref, wb_ref):
    j = pl.program_id(0)

    @pl.when(pl.program_id(1) == 0)
    def _():
        wb_ref[...] = w_ref[...].astype(BF16)

    cw = WIDTH // PROJ_CHUNKS

    def chunks(rotary, scale):
        for c in range(PROJ_CHUNKS):
            cols = slice(c * cw, (c + 1) * cw)
            y = _dot(hb_ref[...], wb_ref[:, cols])
            if rotary:
                cos = jnp.concatenate([cos_ref[...]] * (cw // HEAD_W), axis=1)
                sin = jnp.concatenate([sin_ref[...]] * (cw // HEAD_W), axis=1)
                lane = lax.broadcasted_iota(I32, y.shape, 1)
                first_half = (lane % A_HEAD_DIM) < (A_HEAD_DIM // 2)
                partner = jnp.where(first_half,
                                    pltpu.roll(y, cw - A_HEAD_DIM // 2, 1),
                                    pltpu.roll(y, A_HEAD_DIM // 2, 1))
                y = y * cos + partner * sin
            f_ref[:, cols] = y
            b_ref[:, cols] = (y * scale).astype(BF16)

    @pl.when(j <= SEG_AK)
    def _():
        chunks(True, jnp.where(j == SEG_AQ, qscale[SEG_AQ], 1.0))

    @pl.when(j > SEG_AK)
    def _():
        chunks(False, jnp.where((j == SEG_BQ) | (j == SEG_CQ), qscale[SEG_BQ], 1.0))


def _project(hb, w_in, layer, cos_t, sin_t):
    rows, d = hb.shape
    tm = _row_tile(rows, 640)
    qscale = {SEG_AQ: A_HEAD_DIM ** -0.5, SEG_BQ: HEAD_W ** -0.5}
    return pl.pallas_call(
        functools.partial(_proj_kernel, qscale),
        grid=(N_SEG, rows // tm),
        in_specs=[
            pl.BlockSpec((tm, d), lambda j, i: (i, 0)),
            pl.BlockSpec((None, d, WIDTH), lambda j, i: (layer, 0, j)),
            pl.BlockSpec((tm, HEAD_W), lambda j, i: (i, 0)),
            pl.BlockSpec((tm, HEAD_W), lambda j, i: (i, 0)),
        ],
        out_specs=(pl.BlockSpec((None, tm, WIDTH), lambda j, i: (j, i, 0)),
                   pl.BlockSpec((None, tm, WIDTH), lambda j, i: (j, i, 0))),
        out_shape=(jax.ShapeDtypeStruct((N_SEG, rows, WIDTH), F32),
                   jax.ShapeDtypeStruct((N_SEG, rows, WIDTH), BF16)),
        scratch_shapes=[pltpu.VMEM((d, WIDTH), BF16)],
        compiler_params=_params(("arbitrary", "arbitrary")), name="in_proj")(hb, w_in, cos_t, sin_t)


def _forget_kernel(h_ref, w_ref, bias_ref, o_ref):
    x = _dot3(h_ref[...], w_ref[...]) + bias_ref[...]
    o_ref[...] = jnp.minimum(x, 0.0) - jnp.log(1.0 + jnp.exp(-jnp.abs(x)))


def _forget_logf(h, w_f, b_f):
    rows, d = h.shape
    tm = _row_tile(rows, 512)
    return pl.pallas_call(
        _forget_kernel, grid=(rows // tm,),
        in_specs=[pl.BlockSpec((tm, d), lambda i: (i, 0)),
                  pl.BlockSpec((d, HEAD_W), lambda i: (0, 0)),
                  pl.BlockSpec((1, HEAD_W), lambda i: (0, 0))],
        out_specs=pl.BlockSpec((tm, HEAD_W), lambda i: (i, 0)),
        out_shape=jax.ShapeDtypeStruct((rows, HEAD_W), F32),
        compiler_params=_params(("parallel",)), name="forget_gate")(h, w_f, b_f)


def _cumsum_kernel(x_ref, o_ref):
    x = x_ref[...]
    n = x.shape[1]
    lane = lax.broadcasted_iota(I32, x.shape, 1)
    shift = 1
    while shift < n:
        x = x + jnp.where(lane >= shift, pltpu.roll(x, shift, 1), 0.0)
        shift *= 2
    o_ref[...] = x


def _cumsum_lanes(x):
    return pl.pallas_call(
        _cumsum_kernel, out_shape=jax.ShapeDtypeStruct(x.shape, F32),
        compiler_params=pltpu.CompilerParams(vmem_limit_bytes=VMEM_LIMIT_BYTES), name="cumsum")(x)


def _lanes(x, n):
    if n <= HEAD_W:
        return x[:, :n]
    return jnp.tile(x, (1, n // HEAD_W))


def _with_ones(v):
    return jnp.concatenate([v, jnp.ones_like(v)], axis=1)


def _softmax_block(s, v1, m_ref, acc_ref):
    m_old = m_ref[...]
    m_new = jnp.maximum(m_old, jnp.max(s, axis=-1, keepdims=True))
    p = jnp.exp(s - _lanes(m_new, s.shape[1]))
    a = jnp.exp(m_old - m_new)
    acc_ref[...] = _lanes(a, 2 * HEAD_W) * acc_ref[...] + _dot(p.astype(BF16), v1)
    m_ref[...] = m_new


def _init_stats(m_ref, acc_ref):
    m_ref[...] = jnp.full(m_ref.shape, NEG_INF, F32)
    acc_ref[...] = jnp.zeros(acc_ref.shape, F32)


def _normalized(acc):
    return acc[:, :HEAD_W] / acc[:, HEAD_W:]


def _split_halves(q):
    lane = lax.broadcasted_iota(I32, q.shape, 1)
    zero = jnp.zeros_like(q)
    return jnp.concatenate([jnp.where(lane < A_HEAD_DIM, q, zero),
                            jnp.where(lane >= A_HEAD_DIM, q, zero)], axis=0)


def _diff_lambda(dl, lam_init):
    s1 = jnp.sum(dl[0:1, :] * dl[1:2, :], axis=-1, keepdims=True)
    s2 = jnp.sum(dl[2:3, :] * dl[3:4, :], axis=-1, keepdims=True)
    return jnp.exp(s1) - jnp.exp(s2) + lam_init


def _diff_finish(acc, n, lam, gain):
    o = _normalized(acc[:n]) - lam * _normalized(acc[n:])
    return o * lax.rsqrt(jnp.mean(o * o, axis=-1, keepdims=True) + LN_EPS) * gain


FLASH_TQ = {"diff": 1024, "forget": 1024}
FLASH_BIG = 2048


def _prompt_flash_kernel(kind, tq, big, lam_init, *refs):
    if kind == "diff":
        q_ref, k_ref, v_ref, dl_ref, gain_ref, o_ref, qs_ref, v1_ref, m_ref, acc_ref = refs
    else:
        q_ref, k_ref, v_ref, cq_ref, ck_ref, o_ref, qs_ref, v1_ref, cqr_ref, m_ref, acc_ref = refs
    h = pl.program_id(0)
    i = pl.program_id(1)
    q0 = i * tq
    nq = qs_ref.shape[0]

    @pl.when(i == 0)
    def _():
        v1_ref[...] = _with_ones(v_ref[...])

    if kind == "diff":
        qs_ref[...] = _split_halves(q_ref[...])
    else:
        qs_ref[...] = q_ref[...]
        head_lane = lax.broadcasted_iota(I32, cq_ref.shape, 1) == h
        cq = jnp.sum(jnp.where(head_lane, cq_ref[...], 0.0), axis=-1, keepdims=True)
        cqr_ref[...] = jnp.broadcast_to(cq, cqr_ref.shape)
    _init_stats(m_ref, acc_ref)

    half = nq // 2

    def block(k0, tk, diagonal):
        k = k_ref[pl.ds(k0, tk), :]
        v1 = v1_ref[pl.ds(k0, tk), :]
        scores = []
        for r in (0, half):
            s = _dot_nt(qs_ref[pl.ds(r, half), :], k)
            if kind == "forget":
                s = s + _lanes(cqr_ref[pl.ds(r, half), :], tk) - ck_ref[:, pl.ds(k0, tk)]
            if diagonal:
                qpos = q0 + (r + lax.broadcasted_iota(I32, (half, tk), 0)) % tq
                kpos = k0 + lax.broadcasted_iota(I32, (half, tk), 1)
                visible = (kpos // CHUNK) <= (qpos // CHUNK) if kind == "diff" else kpos <= qpos
                s = jnp.where(visible, s, NEG_INF)
            scores.append(s)
        for r, s in zip((0, half), scores):
            _softmax_block(s, v1, m_ref.at[pl.ds(r, half)], acc_ref.at[pl.ds(r, half)])

    def big_body(kb, carry):
        block(pl.multiple_of(kb * big, big), big, False)
        return carry

    def small_body(kb, carry):
        block(pl.multiple_of(kb * tq, tq), tq, False)
        return carry

    n_big = q0 // big
    lax.fori_loop(0, n_big, big_body, 0)
    lax.fori_loop(n_big * (big // tq), i, small_body, 0)
    block(pl.multiple_of(q0, tq), tq, True)

    if kind == "diff":
        lam = _diff_lambda(dl_ref[...], lam_init)
        o = _diff_finish(acc_ref[...], tq, lam, gain_ref[...])
    else:
        o = _normalized(acc_ref[...])
    o_ref[...] = o.astype(BF16)


def _prompt_flash(kind, qkv, seg_q, t, extra, lam_init=0.0):
    tq = min(FLASH_TQ[kind], t)
    big = min(FLASH_BIG, t)
    assert t % big == 0 and big % tq == 0
    nq = 2 * tq if kind == "diff" else tq
    in_specs = [
        pl.BlockSpec((None, tq, HEAD_W), lambda h, i: (seg_q, i, h)),
        pl.BlockSpec((None, t, HEAD_W), lambda h, i: (seg_q + 1, 0, h)),
        pl.BlockSpec((None, t, HEAD_W), lambda h, i: (seg_q + 2, 0, h)),
    ]
    scratch = [pltpu.VMEM((nq, HEAD_W), BF16), pltpu.VMEM((t, 2 * HEAD_W), BF16)]
    if kind == "diff":
        dl, gain = extra
        in_specs += [pl.BlockSpec(dl.shape, lambda h, i: (0, 0)),
                     pl.BlockSpec(gain.shape, lambda h, i: (0, 0))]
    else:
        cum, cum_t = extra
        in_specs += [pl.BlockSpec((tq, N_HEADS), lambda h, i: (i, 0)),
                     pl.BlockSpec((None, 1, t), lambda h, i: (h, 0, 0))]
        scratch += [pltpu.VMEM((tq, HEAD_W), F32)]
    scratch += [pltpu.VMEM((nq, HEAD_W), F32), pltpu.VMEM((nq, 2 * HEAD_W), F32)]
    return pl.pallas_call(
        functools.partial(_prompt_flash_kernel, kind, tq, big, lam_init),
        grid=(N_HEADS, t // tq), in_specs=in_specs,
        out_specs=pl.BlockSpec((tq, HEAD_W), lambda h, i: (i, h)),
        out_shape=jax.ShapeDtypeStruct((t, WIDTH), BF16),
        scratch_shapes=scratch,
        compiler_params=_params(("arbitrary", "arbitrary")),
        name="prompt_" + kind)(qkv, qkv, qkv, *extra)


BAND_TQ = 256
BAND_PAD = BAND_CHUNKS * CHUNK
BAND_WIN = BAND_PAD + BAND_TQ


def _prompt_band_kernel(q_ref, k_ref, v_ref, bias_ref, o_ref, kp_ref, v1_ref):
    i = pl.program_id(1)

    @pl.when(i == 0)
    def _():
        kp_ref[:BAND_PAD, :] = jnp.zeros((BAND_PAD, HEAD_W), BF16)
        kp_ref[BAND_PAD:, :] = k_ref[...]
        v1_ref[:BAND_PAD, :] = jnp.zeros((BAND_PAD, 2 * HEAD_W), BF16)
        v1_ref[BAND_PAD:, :] = _with_ones(v_ref[...])

    r0 = pl.multiple_of(i * BAND_TQ, BAND_TQ)
    k = kp_ref[pl.ds(r0, BAND_WIN), :]
    v1 = v1_ref[pl.ds(r0, BAND_WIN), :]
    half = BAND_TQ // 2
    in_range = r0 - BAND_PAD + lax.broadcasted_iota(I32, (half, BAND_WIN), 1) >= 0
    scores = [jnp.where(in_range, _dot_nt(q_ref[pl.ds(r, half), :], k) + bias_ref[pl.ds(r, half), :], NEG_INF)
              for r in (0, half)]
    for r, s in zip((0, half), scores):
        m = jnp.broadcast_to(jnp.max(s, axis=-1, keepdims=True), (half, HEAD_W))
        p = jnp.exp(s - _lanes(m, BAND_WIN))
        o_ref[pl.ds(r, half), :] = _normalized(_dot(p.astype(BF16), v1)).astype(BF16)


def _band_bias_window(rel_bias_l):
    n_tab = rel_bias_l.shape[0]
    span = BAND_TQ + BAND_WIN - 1
    dist = BAND_PAD + (BAND_TQ - 1) - np.arange(span)
    e = rel_bias_l[:, np.clip(dist, -REL_CLIP, REL_CLIP) + REL_CLIP]
    period = span + 1
    f = jnp.concatenate([e[:, BAND_TQ - 1:], jnp.zeros((n_tab, 1), e.dtype), e[:, :BAND_TQ - 1]], axis=1)
    skew = jnp.tile(f, (1, BAND_TQ))[:, :BAND_TQ * span].reshape(n_tab, BAND_TQ, span)
    bias = skew[:, :, :BAND_WIN]
    r = np.arange(BAND_TQ)[:, None]
    c = np.arange(BAND_WIN)[None, :] - BAND_PAD
    visible = (c // CHUNK <= r // CHUNK) & (c // CHUNK >= r // CHUNK - BAND_CHUNKS)
    assert period == span + 1
    return jnp.where(visible[None], bias, NEG_INF).astype(F32)


def _prompt_band(qkv, t, bias):
    assert t % BAND_TQ == 0 and BAND_TQ % CHUNK == 0
    return pl.pallas_call(
        _prompt_band_kernel, grid=(N_HEADS, t // BAND_TQ),
        in_specs=[
            pl.BlockSpec((None, BAND_TQ, HEAD_W), lambda h, i: (SEG_BQ, i, h)),
            pl.BlockSpec((None, t, HEAD_W), lambda h, i: (SEG_BK, 0, h)),
            pl.BlockSpec((None, t, HEAD_W), lambda h, i: (SEG_BV, 0, h)),
            pl.BlockSpec((None, BAND_TQ, BAND_WIN), lambda h, i: (h, 0, 0)),
        ],
        out_specs=pl.BlockSpec((BAND_TQ, HEAD_W), lambda h, i: (i, h)),
        out_shape=jax.ShapeDtypeStruct((t, WIDTH), BF16),
        scratch_shapes=[pltpu.VMEM((t + BAND_PAD, HEAD_W), BF16), pltpu.VMEM((t + BAND_PAD, 2 * HEAD_W), BF16)],
        compiler_params=_params(("arbitrary", "arbitrary")), name="prompt_band")(qkv, qkv, qkv, bias)


def _sample_attn_kernel(kind, layer, n_kt, tk, ts, lam_init, *refs):
    *refs, kbuf_ref, vbuf_ref, sem_ref = refs
    if kind == "diff":
        (q_ref, kc_hbm, vc_hbm, kn_ref, vn_ref, dl_ref, gain_ref, o_ref, qs_ref, m_ref, acc_ref) = refs
    elif kind == "band":
        (q_ref, kc_hbm, vc_hbm, kn_ref, vn_ref, bc_ref, bn_ref, o_ref, qs_ref, m_ref, acc_ref) = refs
    else:
        (q_ref, kc_hbm, vc_hbm, kn_ref, vn_ref, cq_ref, ckc_ref, ckn_ref, o_ref, qs_ref, m_ref, acc_ref) = refs
    kt = pl.program_id(1)
    nq = qs_ref.shape[1]
    step = pl.program_id(0) * n_kt + kt
    n_steps = pl.num_programs(0) * n_kt
    slot = step % 2

    def head_copies(stream, tile, buf):
        rows = pl.ds(pl.multiple_of(tile * tk, tk), tk)
        copies = []
        for h in range(N_HEADS):
            copies.append(pltpu.make_async_copy(kc_hbm.at[layer, stream, rows, h, :], kbuf_ref.at[buf, h],
                                                sem_ref.at[0, buf]))
            copies.append(pltpu.make_async_copy(vc_hbm.at[layer, stream, rows, h, :], vbuf_ref.at[buf, h],
                                                sem_ref.at[1, buf]))
        return copies

    @pl.when(step == 0)
    def _():
        for c in head_copies(0, 0, 0):
            c.start()

    @pl.when(step + 1 < n_steps)
    def _():
        for c in head_copies((step + 1) // n_kt, (step + 1) % n_kt, 1 - slot):
            c.start()

    for c in head_copies(pl.program_id(0), kt, slot):
        c.wait()

    @pl.when(kt == 0)
    def _():
        for h in range(N_HEADS):
            q = q_ref[:, h * HEAD_W:(h + 1) * HEAD_W]
            qs_ref[h] = _split_halves(q) if kind == "diff" else q
        _init_stats(m_ref, acc_ref)

    def head_bias(h, s, new):
        if kind == "band":
            return s + (bn_ref[h] if new else bc_ref[h])
        if kind == "forget":
            cq = cq_ref[:, h:h + 1]
            ck = ckn_ref[h:h + 1, :] if new else ckc_ref[h:h + 1, :]
            return s + cq - ck
        return s

    for h in range(N_HEADS):
        s = _dot_nt(qs_ref[h], kbuf_ref[slot, h].astype(BF16))
        _softmax_block(head_bias(h, s, False), _with_ones(vbuf_ref[slot, h].astype(BF16)),
                       m_ref.at[h], acc_ref.at[h])

    @pl.when(kt == n_kt - 1)
    def _():
        for h in range(N_HEADS):
            sl = slice(h * HEAD_W, (h + 1) * HEAD_W)
            s = head_bias(h, _dot_nt(qs_ref[h], kn_ref[:, sl]), True)
            if kind == "forget":
                r = lax.broadcasted_iota(I32, (nq, ts), 0)
                c = lax.broadcasted_iota(I32, (nq, ts), 1)
                s = jnp.where(c <= r, s, NEG_INF)
            _softmax_block(s, _with_ones(vn_ref[:, sl]), m_ref.at[h], acc_ref.at[h])
            if kind == "diff":
                lam = _diff_lambda(dl_ref[...], lam_init)
                o = _diff_finish(acc_ref[h], ts, lam, gain_ref[...])
            else:
                o = _normalized(acc_ref[h])
            o_ref[:, sl] = o.astype(BF16)


def _sample_attn(kind, qkv, seg_q, cache_k, cache_v, layer, t_prompt, extra, lam_init=0.0):
    _, n_streams, past = cache_k.shape[:3]
    rows = qkv.shape[1]
    ts = (rows - t_prompt) // n_streams
    assert t_prompt % ts == 0 and ts % 16 == 0
    tk = _row_tile(past, 1024, 128)
    n_kt = past // tk
    nq = 2 * ts if kind == "diff" else ts
    row0 = t_prompt // ts
    new_spec = lambda seg: pl.BlockSpec((None, ts, WIDTH), lambda b, kt: (seg, row0 + b, 0))
    cache_spec = pl.BlockSpec(memory_space=pl.ANY)
    in_specs = [new_spec(seg_q), cache_spec, cache_spec, new_spec(seg_q + 1), new_spec(seg_q + 2)]
    if kind == "diff":
        dl, gain = extra
        in_specs += [pl.BlockSpec(dl.shape, lambda b, kt: (0, 0)),
                     pl.BlockSpec(gain.shape, lambda b, kt: (0, 0))]
    elif kind == "band":
        bias_c, bias_n = extra
        in_specs += [pl.BlockSpec((N_HEADS, ts, tk), lambda b, kt: (0, 0, kt)),
                     pl.BlockSpec(bias_n.shape, lambda b, kt: (0, 0, 0))]
    else:
        cq, ck_cache, ck_new = extra
        in_specs += [pl.BlockSpec((None, ts, N_HEADS), lambda b, kt: (b, 0, 0)),
                     pl.BlockSpec((None, N_HEADS, tk), lambda b, kt: (b, 0, kt)),
                     pl.BlockSpec((None, N_HEADS, ts), lambda b, kt: (b, 0, 0))]
    return pl.pallas_call(
        functools.partial(_sample_attn_kernel, kind, layer, n_kt, tk, ts, lam_init),
        grid=(n_streams, n_kt), in_specs=in_specs,
        out_specs=pl.BlockSpec((ts, WIDTH), lambda b, kt: (b, 0)),
        out_shape=jax.ShapeDtypeStruct((n_streams * ts, WIDTH), BF16),
        scratch_shapes=[pltpu.VMEM((N_HEADS, nq, HEAD_W), BF16), pltpu.VMEM((N_HEADS, nq, HEAD_W), F32),
                        pltpu.VMEM((N_HEADS, nq, 2 * HEAD_W), F32),
                        pltpu.VMEM((2, N_HEADS, tk, HEAD_W), F32), pltpu.VMEM((2, N_HEADS, tk, HEAD_W), F32),
                        pltpu.SemaphoreType.DMA((2, 2))],
        compiler_params=_params(("arbitrary", "arbitrary")),
        name="sample_" + kind)(qkv, cache_k, cache_v, qkv, qkv, *extra)


def _mix_kernel(hb_ref, oa_ref, ob_ref, oc_ref, wg0_ref, wg1_ref, wg2_ref, bg_ref,
                wa_ref, wb_ref, wc_ref, o_ref, wgs_ref, wbs_ref):
    @pl.when(pl.program_id(1) == 0)
    def _():
        for b, (wg, wbr) in enumerate(((wg0_ref, wa_ref), (wg1_ref, wb_ref), (wg2_ref, wc_ref))):
            wgs_ref[b] = wg[...].astype(BF16)
            wbs_ref[b] = wbr[...].astype(BF16)

    hb = hb_ref[...]
    mixed = None
    for b, o in enumerate((oa_ref, ob_ref, oc_ref)):
        gate = jax.nn.sigmoid(_dot(hb, wgs_ref[b]) + bg_ref[b:b + 1, :])
        term = gate * _dot(o[...], wbs_ref[b])
        mixed = term if mixed is None else mixed + term
    o_ref[...] = mixed.astype(BF16)


def _mix(hb, o_a, o_b, o_c, w_gate, b_gate_l, w_a, w_b, w_c, layer):
    rows, d = hb.shape
    tm = _row_tile(rows, 640)
    tn = 256
    nb = d // tn
    gate_spec = lambda b: pl.BlockSpec((None, d, tn), lambda n, i: (layer, 0, b * nb + n))
    branch_spec = pl.BlockSpec((None, WIDTH, tn), lambda n, i: (layer, 0, n))
    o_spec = pl.BlockSpec((tm, WIDTH), lambda n, i: (i, 0))
    return pl.pallas_call(
        _mix_kernel, grid=(nb, rows // tm),
        in_specs=[pl.BlockSpec((tm, d), lambda n, i: (i, 0)), o_spec, o_spec, o_spec,
                  gate_spec(0), gate_spec(1), gate_spec(2),
                  pl.BlockSpec((N_BRANCHES, tn), lambda n, i: (0, n)),
                  branch_spec, branch_spec, branch_spec],
        out_specs=pl.BlockSpec((tm, tn), lambda n, i: (i, n)),
        out_shape=jax.ShapeDtypeStruct((rows, d), BF16),
        scratch_shapes=[pltpu.VMEM((N_BRANCHES, d, tn), BF16), pltpu.VMEM((N_BRANCHES, WIDTH, tn), BF16)],
        compiler_params=_params(("arbitrary", "arbitrary")),
        name="branch_mix")(hb, o_a, o_b, o_c, w_gate, w_gate, w_gate, b_gate_l, w_a, w_b, w_c)


def _matmul_kernel(x_ref, w_ref, o_ref, wb_ref):
    @pl.when(pl.program_id(1) == 0)
    def _():
        wb_ref[...] = w_ref[...].astype(BF16)
    o_ref[...] = _dot(x_ref[...], wb_ref[...])


def _matmul(x, w, layer):
    rows, k = x.shape
    n = w.shape[-1]
    tm = _row_tile(rows, 640)
    tn = min(512, n)
    assert n % tn == 0
    return pl.pallas_call(
        _matmul_kernel, grid=(n // tn, rows // tm),
        in_specs=[pl.BlockSpec((tm, k), lambda j, i: (i, 0)),
                  pl.BlockSpec((None, k, tn), lambda j, i: (layer, 0, j))],
        out_specs=pl.BlockSpec((tm, tn), lambda j, i: (i, j)),
        out_shape=jax.ShapeDtypeStruct((rows, n), F32),
        scratch_shapes=[pltpu.VMEM((k, tn), BF16)],
        compiler_params=_params(("arbitrary", "arbitrary")), name="out_proj")(x, w)


def _router_kernel(h_ref, w_ref, b_ref, idx_ref, gate_ref, rank_ref, cnt_ref, run_ref):
    @pl.when(pl.program_id(0) == 0)
    def _():
        run_ref[...] = jnp.zeros(run_ref.shape, F32)

    logits = _dot3(h_ref[...], w_ref[...]) + b_ref[...]
    tm, n_e = logits.shape
    e_iota = lax.broadcasted_iota(I32, (tm, n_e), 1)
    out_lane = lax.broadcasted_iota(I32, (tm, HEAD_W), 1)
    vals, hots, idxs = [], [], []
    for _ in range(TOP_K):
        top = jnp.max(logits, axis=-1, keepdims=True)
        idx = jnp.min(jnp.where(logits == top, e_iota, n_e), axis=-1, keepdims=True)
        hot = e_iota == idx
        vals.append(top)
        idxs.append(idx)
        hots.append(hot)
        logits = jnp.where(hot, -jnp.inf, logits)
    exps = [jnp.exp(v - vals[0]) for v in vals]
    denom = exps[0] + exps[1] + exps[2] + exps[3]

    chosen = hots[0] | hots[1] | hots[2] | hots[3]
    chosen_f = jnp.where(chosen, 1.0, 0.0)
    r = lax.broadcasted_iota(I32, (tm, tm), 0)
    c = lax.broadcasted_iota(I32, (tm, tm), 1)
    before = jnp.where(c < r, 1.0, 0.0).astype(BF16)
    rank = _dot(before, chosen_f.astype(BF16)) + run_ref[...]
    run_ref[...] = run_ref[...] + jnp.sum(chosen_f, axis=0, keepdims=True)

    idx_out = jnp.zeros((tm, HEAD_W), I32)
    gate_out = jnp.zeros((tm, HEAD_W), F32)
    rank_out = jnp.zeros((tm, HEAD_W), I32)
    for k in range(TOP_K):
        rk = jnp.sum(jnp.where(hots[k], rank, 0.0), axis=-1, keepdims=True).astype(I32)
        idx_out = jnp.where(out_lane == k, idxs[k], idx_out)
        gate_out = jnp.where(out_lane == k, exps[k] / denom, gate_out)
        rank_out = jnp.where(out_lane == k, rk, rank_out)
    idx_ref[...] = idx_out
    gate_ref[...] = gate_out
    rank_ref[...] = rank_out
    cnt_ref[...] = run_ref[...].astype(I32)


def _router(h, w_router_l, b_router_l):
    rows, d = h.shape
    n_e = w_router_l.shape[-1]
    tm = _row_tile(rows, 256)
    lanes = pl.BlockSpec((tm, HEAD_W), lambda i: (i, 0))
    return pl.pallas_call(
        _router_kernel, grid=(rows // tm,),
        in_specs=[pl.BlockSpec((tm, d), lambda i: (i, 0)),
                  pl.BlockSpec((d, n_e), lambda i: (0, 0)),
                  pl.BlockSpec((1, n_e), lambda i: (0, 0))],
        out_specs=(lanes, lanes, lanes, pl.BlockSpec((1, n_e), lambda i: (0, 0))),
        out_shape=(jax.ShapeDtypeStruct((rows, HEAD_W), I32), jax.ShapeDtypeStruct((rows, HEAD_W), F32),
                   jax.ShapeDtypeStruct((rows, HEAD_W), I32), jax.ShapeDtypeStruct((1, n_e), I32)),
        scratch_shapes=[pltpu.VMEM((1, n_e), F32)],
        compiler_params=_params(("arbitrary",)), name="router")(h, w_router_l, b_router_l.reshape(1, n_e))


MOE_TM = 256
COMBINE_TM = 128


def _expert_changed(te_ref, i):
    return (i == 0) | (te_ref[i] != te_ref[jnp.maximum(i - 1, 0)])


def _gather_rows(src_hbm, idx_ref, base, n_rows, dst, sem):
    for r in range(n_rows):
        pltpu.make_async_copy(src_hbm.at[pl.ds(idx_ref[base + r], 1)], dst.at[pl.ds(r, 1)], sem).start()


def _wait_rows(src_hbm, n_rows, dst, sem):
    pltpu.make_async_copy(src_hbm.at[pl.ds(0, n_rows)], dst, sem).wait()


def _weight_copy(w_hbm, layer, expert, col0, tn, stage_ref, sem):
    return pltpu.make_async_copy(w_hbm.at[layer, expert, :, pl.ds(col0, tn)], stage_ref, sem)


def _staged_weights(copies_for, te_ref, nx_ref, n, i, live, cast):
    @pl.when((n == 0) & (i == 0))
    def _():
        for c in copies_for(te_ref[0], 0):
            c.start()

    @pl.when(live & _expert_changed(te_ref, i))
    def _():
        for c in copies_for(te_ref[i], n):
            c.wait()
        cast()
        nxt = nx_ref[i]
        more = nxt < N_EXPERTS

        @pl.when(more)
        def _():
            for c in copies_for(nxt, n):
                c.start()

        @pl.when(jnp.logical_not(more) & (n + 1 < pl.num_programs(0)))
        def _():
            for c in copies_for(te_ref[0], n + 1):
                c.start()


def _gmm1_kernel(layer, tn, ff, te_ref, nx_ref, na_ref, tok_ref, h_hbm, w_hbm, bg_ref, bu_ref, o_ref,
                 wgs_ref, wus_ref, stg_ref, stu_ref, xbuf_ref, xsem_ref, wsem_ref):
    n = pl.program_id(0)
    i = pl.program_id(1)
    n_active = na_ref[0]
    live = i < n_active
    slot = i % 2

    def copies_for(expert, sweep):
        c0 = pl.multiple_of(sweep * tn, tn)
        return (_weight_copy(w_hbm, layer, expert, c0, tn, stg_ref, wsem_ref.at[0]),
                _weight_copy(w_hbm, layer, expert, ff + c0, tn, stu_ref, wsem_ref.at[1]))

    def cast():
        wgs_ref[...] = stg_ref[...].astype(BF16)
        wus_ref[...] = stu_ref[...].astype(BF16)

    @pl.when(i == 0)
    def _():
        _gather_rows(h_hbm, tok_ref, 0, MOE_TM, xbuf_ref.at[0], xsem_ref.at[0])

    _staged_weights(copies_for, te_ref, nx_ref, n, i, live, cast)

    @pl.when(live)
    def _():
        _wait_rows(h_hbm, MOE_TM, xbuf_ref.at[slot], xsem_ref.at[slot])
        ahead = jnp.minimum(i + 1, n_active - 1)
        _gather_rows(h_hbm, tok_ref, ahead * MOE_TM, MOE_TM, xbuf_ref.at[1 - slot], xsem_ref.at[1 - slot])
        x = xbuf_ref[slot].astype(BF16)
        g = jnp.minimum(_dot(x, wgs_ref[...]) + bg_ref[...], SWIGLU_LIMIT)
        u = jnp.clip(_dot(x, wus_ref[...]) + bu_ref[...], -SWIGLU_LIMIT, SWIGLU_LIMIT)
        o_ref[...] = (g * jax.nn.sigmoid(SWIGLU_ALPHA * g) * (u + 1.0)).astype(BF16)

    @pl.when(i == n_active - 1)
    def _():
        _wait_rows(h_hbm, MOE_TM, xbuf_ref.at[1 - slot], xsem_ref.at[1 - slot])

    @pl.when(jnp.logical_not(live))
    def _():
        o_ref[...] = jnp.zeros(o_ref.shape, o_ref.dtype)


def _gmm2_kernel(layer, tn, te_ref, nx_ref, na_ref, x_ref, w_hbm, b_ref, o_ref, ws_ref, st_ref, wsem_ref):
    n = pl.program_id(0)
    i = pl.program_id(1)
    live = i < na_ref[0]

    def copies_for(expert, sweep):
        return (_weight_copy(w_hbm, layer, expert, pl.multiple_of(sweep * tn, tn), tn, st_ref, wsem_ref.at[0]),)

    def cast():
        ws_ref[...] = st_ref[...].astype(BF16)

    _staged_weights(copies_for, te_ref, nx_ref, n, i, live, cast)

    @pl.when(live)
    def _():
        o_ref[...] = _dot(x_ref[...], ws_ref[...]) + b_ref[...]

    @pl.when(jnp.logical_not(live))
    def _():
        o_ref[...] = jnp.zeros(o_ref.shape, o_ref.dtype)


def _experts(h, row_tok, tile_expert, next_expert, n_active, w_gate_up, b_gate_up_l, w_down, b_down_l, layer):
    n_rows = row_tok.shape[0]
    d = h.shape[1]
    ff = w_down.shape[2]
    n_tiles = n_rows // MOE_TM
    tn = min(1024, ff)
    nff = ff // tn
    tile = lambda i, na: jnp.minimum(i, na[0] - 1)
    act = pl.pallas_call(
        functools.partial(_gmm1_kernel, layer, tn, ff),
        grid_spec=pltpu.PrefetchScalarGridSpec(
            num_scalar_prefetch=4, grid=(nff, n_tiles),
            in_specs=[
                pl.BlockSpec(memory_space=pl.ANY),
                pl.BlockSpec(memory_space=pl.ANY),
                pl.BlockSpec((None, 1, tn), lambda n, i, te, nx, na, tok: (te[i], 0, n)),
                pl.BlockSpec((None, 1, tn), lambda n, i, te, nx, na, tok: (te[i], 0, nff + n)),
            ],
            out_specs=pl.BlockSpec((MOE_TM, tn), lambda n, i, te, nx, na, tok: (i, n)),
            scratch_shapes=[pltpu.VMEM((d, tn), BF16), pltpu.VMEM((d, tn), BF16),
                            pltpu.VMEM((d, tn), F32), pltpu.VMEM((d, tn), F32),
                            pltpu.VMEM((2, MOE_TM, d), F32),
                            pltpu.SemaphoreType.DMA((2,)), pltpu.SemaphoreType.DMA((2,))]),
        out_shape=jax.ShapeDtypeStruct((n_rows, ff), BF16),
        compiler_params=_params(("arbitrary", "arbitrary")),
        name="expert_up")(tile_expert, next_expert, n_active, row_tok, h, w_gate_up, b_gate_up_l, b_gate_up_l)
    tn2 = min(1024, d)
    return pl.pallas_call(
        functools.partial(_gmm2_kernel, layer, tn2),
        grid_spec=pltpu.PrefetchScalarGridSpec(
            num_scalar_prefetch=3, grid=(d // tn2, n_tiles),
            in_specs=[
                pl.BlockSpec((MOE_TM, ff), lambda n, i, te, nx, na: (tile(i, na), 0)),
                pl.BlockSpec(memory_space=pl.ANY),
                pl.BlockSpec((None, 1, tn2), lambda n, i, te, nx, na: (te[i], 0, n)),
            ],
            out_specs=pl.BlockSpec((MOE_TM, tn2), lambda n, i, te, nx, na: (i, n)),
            scratch_shapes=[pltpu.VMEM((ff, tn2), BF16), pltpu.VMEM((ff, tn2), F32),
                            pltpu.SemaphoreType.DMA((1,))]),
        out_shape=jax.ShapeDtypeStruct((n_rows, d), F32),
        compiler_params=_params(("arbitrary", "arbitrary")),
        name="expert_down")(tile_expert, next_expert, n_active, act, w_down, b_down_l)


def _combine_ln_kernel(alpha, n_steps, dest_ref, h_ref, gate_ref, g_ref, b_ref, y_hbm, o_ref, ob_ref,
                       ybuf_ref, sem_ref):
    i = pl.program_id(0)
    slot = i % 2
    n_gather = TOP_K * COMBINE_TM

    @pl.when(i == 0)
    def _():
        _gather_rows(y_hbm, dest_ref, 0, n_gather, ybuf_ref.at[0], sem_ref.at[0])

    @pl.when(i + 1 < n_steps)
    def _():
        _gather_rows(y_hbm, dest_ref, (i + 1) * n_gather, n_gather, ybuf_ref.at[1 - slot], sem_ref.at[1 - slot])

    _wait_rows(y_hbm, n_gather, ybuf_ref.at[slot], sem_ref.at[slot])
    moe = None
    for k in range(TOP_K):
        term = gate_ref[:, k:k + 1] * ybuf_ref[slot, pl.ds(k * COMBINE_TM, COMBINE_TM), :]
        moe = term if moe is None else moe + term
    y = _ln_rows(alpha * h_ref[...] + moe, g_ref[...], b_ref[...])
    o_ref[...] = y
    ob_ref[...] = y.astype(BF16)


def _combine_ln(h, y_rows, dest_km, gate, g, b, alpha):
    rows, d = h.shape
    n_steps = rows // COMBINE_TM
    row_spec = pl.BlockSpec((COMBINE_TM, d), lambda i, dest: (i, 0))
    vec_spec = pl.BlockSpec((1, d), lambda i, dest: (0, 0))
    return pl.pallas_call(
        functools.partial(_combine_ln_kernel, alpha, n_steps),
        grid_spec=pltpu.PrefetchScalarGridSpec(
            num_scalar_prefetch=1, grid=(n_steps,),
            in_specs=[row_spec, pl.BlockSpec((COMBINE_TM, HEAD_W), lambda i, dest: (i, 0)),
                      vec_spec, vec_spec, pl.BlockSpec(memory_space=pl.ANY)],
            out_specs=(row_spec, row_spec),
            scratch_shapes=[pltpu.VMEM((2, TOP_K * COMBINE_TM, d), F32), pltpu.SemaphoreType.DMA((2,))]),
        out_shape=(jax.ShapeDtypeStruct((rows, d), F32), jax.ShapeDtypeStruct((rows, d), BF16)),
        compiler_params=_params(("arbitrary",)),
        name="combine_ln")(dest_km, h, gate, g.reshape(1, d), b.reshape(1, d), y_rows)


def _moe_ln(h, w_router_l, b_router_l, w_gate_up, b_gate_up_l, w_down, b_down_l, layer, g, b, alpha):
    rows, d = h.shape
    n_e = w_router_l.shape[-1]
    assert rows % COMBINE_TM == 0
    idx, gate, rank, counts = _router(h, w_router_l, b_router_l)
    idx, rank, counts = idx[:, :TOP_K], rank[:, :TOP_K], counts[0]
    padded = (counts + MOE_TM - 1) // MOE_TM * MOE_TM
    group_end = jnp.cumsum(padded)
    group_start = group_end - padded
    dest = group_start[idx] + rank
    n_tiles = (rows * TOP_K + n_e * (MOE_TM - 1)) // MOE_TM + 1
    n_rows = n_tiles * MOE_TM
    n_active = jnp.maximum(group_end[-1] // MOE_TM, 1).astype(I32)
    tile_ids = jnp.minimum(jnp.arange(n_tiles, dtype=I32), n_active - 1)
    tile_expert = jnp.sum((group_end[None, :] <= (tile_ids * MOE_TM)[:, None]).astype(I32), axis=1)
    tile_expert = jnp.minimum(tile_expert, n_e - 1)
    present = jnp.where(counts > 0, jnp.arange(n_e, dtype=I32), n_e)
    later = lax.cummin(present, axis=0, reverse=True)
    next_expert = jnp.concatenate([later[1:], jnp.full((1,), n_e, I32)])[tile_expert]
    tok = jnp.broadcast_to(jnp.arange(rows, dtype=I32)[:, None], (rows, TOP_K))
    row_tok = jnp.zeros((n_rows,), I32).at[dest.reshape(-1)].set(tok.reshape(-1))
    y_rows = _experts(h, row_tok, tile_expert, next_expert, n_active.reshape(1), w_gate_up,
                      b_gate_up_l.reshape(n_e, 1, -1), w_down, b_down_l.reshape(n_e, 1, -1), layer)
    dest_km = jnp.swapaxes(dest.reshape(rows // COMBINE_TM, COMBINE_TM, TOP_K), 1, 2).reshape(-1)
    return _combine_ln(h, y_rows, dest_km.astype(I32), gate, g, b, alpha)


def _rotary_tables(pos):
    half = A_HEAD_DIM // 2
    inv = ROPE_THETA ** (-jnp.arange(half, dtype=F32) / half)
    ang = pos.astype(F32)[:, None] * inv[None, :]
    cos, sin = jnp.cos(ang), jnp.sin(ang)
    reps = HEAD_W // A_HEAD_DIM
    return (jnp.concatenate([cos, cos] * reps, axis=1), jnp.concatenate([-sin, sin] * reps, axis=1))


def _sample_band_bias(rel_bias_l, q_pos, k_pos):
    rel = np.clip(q_pos[:, None] - k_pos[None, :], -REL_CLIP, REL_CLIP) + REL_CLIP
    qc, kc = q_pos[:, None] // CHUNK, k_pos[None, :] // CHUNK
    visible = (kc <= qc) & (kc >= qc - BAND_CHUNKS) & (k_pos[None, :] >= 0)
    return jnp.where(visible[None], rel_bias_l[:, rel], NEG_INF).astype(F32)


def kernel(x_prompt, x_sample, cache_a_k, cache_a_v, cache_b_k, cache_b_v, cache_c_k, cache_c_v, cache_c_logf, ln_in_g, ln_in_b, w_in, b_forget, diff_lambda, diff_subln_g, rel_bias, w_branch_a, w_branch_b, w_branch_c, w_gate, b_gate, w_out, ln_mix_g, ln_mix_b, w_router, b_router, w_gate_up, b_gate_up, w_down, b_down, ln_ffn_g, ln_ffn_b):
    n_p, t_p, d = x_prompt.shape
    n_s, t_s, _ = x_sample.shape
    depth, _, past = cache_a_k.shape[:3]
    band = cache_b_k.shape[2]
    assert n_p == 1 and past % CHUNK == 0 and t_s <= CHUNK and band == BAND_CHUNKS * CHUNK
    rows_s = n_s * t_s
    alpha = (2.0 * depth) ** 0.25

    x = jnp.concatenate([x_prompt.reshape(t_p, d), x_sample.reshape(rows_s, d)], axis=0)
    pos = jnp.concatenate([jnp.arange(t_p), jnp.tile(past + jnp.arange(t_s), n_s)])
    cos_t, sin_t = _rotary_tables(pos)
    q_pos_s = past + np.arange(t_s)
    k_pos_band = past - band + np.arange(band + t_s)
    pad_t = (-(past + t_s)) % HEAD_W

    h, hb = _layer_norm(x, ln_in_g, ln_in_b)
    states = []
    for l in range(depth):
        lam_init = 0.8 - 0.6 * math.exp(-0.3 * l)
        gain = (diff_subln_g[l] * (1.0 - lam_init)).reshape(1, HEAD_W)
        dl = diff_lambda[l]

        seg_f, seg_b = _project(hb, w_in, l, cos_t, sin_t)
        w_f = jnp.pad(w_in[l, :, N_SEG * WIDTH:], ((0, 0), (0, HEAD_W - N_HEADS)))
        b_f = jnp.pad(b_forget[l], (0, HEAD_W - N_HEADS)).reshape(1, HEAD_W)
        logf = _forget_logf(h, w_f, b_f)[:, :N_HEADS]
        logf_p, logf_s = logf[:t_p], logf[t_p:].reshape(n_s, t_s, N_HEADS)

        cum_p_t = _cumsum_lanes(logf_p.T)
        o_a_p = _prompt_flash("diff", seg_b, SEG_AQ, t_p, (dl, gain), lam_init)
        o_b_p = _prompt_band(seg_b, t_p, _band_bias_window(rel_bias[l]))
        o_c_p = _prompt_flash("forget", seg_b, SEG_CQ, t_p,
                              (cum_p_t.T, cum_p_t.reshape(N_HEADS, 1, t_p)))

        logf_all = jnp.concatenate([cache_c_logf[l], logf_s], axis=1)
        logf_all = jnp.pad(jnp.swapaxes(logf_all, 1, 2), ((0, 0), (0, 0), (0, pad_t)))
        cum_s = _cumsum_lanes(logf_all.reshape(n_s * N_HEADS, -1)).reshape(n_s, N_HEADS, -1)
        cum_new = cum_s[:, :, past:past + t_s]
        bias_s = _sample_band_bias(rel_bias[l], q_pos_s, k_pos_band)
        o_a_s = _sample_attn("diff", seg_b, SEG_AQ, cache_a_k, cache_a_v, l, t_p, (dl, gain), lam_init)
        o_b_s = _sample_attn("band", seg_b, SEG_BQ, cache_b_k, cache_b_v, l, t_p,
                             (bias_s[:, :, :band], bias_s[:, :, band:]))
        o_c_s = _sample_attn("forget", seg_b, SEG_CQ, cache_c_k, cache_c_v, l, t_p,
                             (jnp.swapaxes(cum_new, 1, 2), cum_s[:, :, :past], cum_new))

        o_a = jnp.concatenate([o_a_p, o_a_s], axis=0)
        o_b = jnp.concatenate([o_b_p, o_b_s], axis=0)
        o_c = jnp.concatenate([o_c_p, o_c_s], axis=0)
        mixed = _mix(hb, o_a, o_b, o_c, w_gate, b_gate[l].reshape(N_BRANCHES, d),
                     w_branch_a, w_branch_b, w_branch_c, l)
        h, hb = _layer_norm(_matmul(mixed, w_out, l), ln_mix_g[l], ln_mix_b[l], h, alpha)
        h, hb = _moe_ln(h, w_router[l], b_router[l], w_gate_up, b_gate_up[l], w_down, b_down[l], l,
                        ln_ffn_g[l], ln_ffn_b[l], alpha)
        states.append((seg_f, logf_p, logf_s))

    def prompt_state(seg, keep=None):
        out = jnp.stack([st[0][seg, :t_p] for st in states]).reshape(depth, 1, t_p, N_HEADS, HEAD_W)
        return out if keep is None else out[:, :, -keep:]

    def sample_state(seg):
        return jnp.stack([st[0][seg, t_p:] for st in states]).reshape(depth, n_s, t_s, N_HEADS, HEAD_W)

    band_p = min(BAND_CHUNKS * CHUNK, t_p)
    return (h[:t_p].reshape(1, t_p, d), h[t_p:].reshape(n_s, t_s, d),
            prompt_state(SEG_AK), prompt_state(SEG_AV),
            prompt_state(SEG_BK, band_p), prompt_state(SEG_BV, band_p),
            prompt_state(SEG_CK), prompt_state(SEG_CV),
            jnp.stack([st[1] for st in states]).reshape(depth, 1, t_p, N_HEADS),
            sample_state(SEG_AK), sample_state(SEG_AV), sample_state(SEG_BK), sample_state(SEG_BV),
            sample_state(SEG_CK), sample_state(SEG_CV),
            jnp.stack([st[2] for st in states]))
```
